```python
import math
import jax, jax.numpy as jnp
from jax import lax
import numpy as np

D_MODEL = 2048
BATCH = 8
SEQ = 2048
DEPTH = 2

N_A_LAYERS = DEPTH // 2
N_B_LAYERS = DEPTH - N_A_LAYERS
NORM_EPS = 1e-6

M_EXPAND = 2
M_D_INNER = M_EXPAND * D_MODEL
M_HEAD_DIM = 64
M_HEADS = M_D_INNER // M_HEAD_DIM
M_D_STATE = 128
M_GROUPS = 8
M_HEADS_PER_GROUP = M_HEADS // M_GROUPS
M_D_CONV = 4
M_CONV_DIM = M_D_INNER + 2 * M_GROUPS * M_D_STATE
M_IN_PROJ = 2 * M_D_INNER + 2 * M_GROUPS * M_D_STATE + M_HEADS
M_CHUNK = 128
M_NORM_EPS = 1e-5

SB_HEAD_DIM = 128
SB_HEADS = D_MODEL // SB_HEAD_DIM
SB_WIDTH = SB_HEADS * SB_HEAD_DIM
SB_BLOCK = 128
SB_SCALE = 1.0 / math.sqrt(SB_HEAD_DIM)

PEER_HEADS = 8
PEER_N_KEYS = 128
PEER_EXPERTS = PEER_N_KEYS * PEER_N_KEYS
PEER_TOPK = 16
PEER_KEY_DIM = 256
PEER_HALF = PEER_KEY_DIM // 2
PEER_TOKEN_BLOCK = 128

kernel_name = "yoco_mamba2_stickbreak_peer"


def rms_norm(x, g, eps=NORM_EPS):
    xf = x.astype(jnp.float32)
    out = xf * lax.rsqrt(jnp.mean(xf * xf, axis=-1, keepdims=True) + eps)
    return (out * g.astype(jnp.float32)).astype(x.dtype)


def gated_group_rms_norm(y, z, w):
    yf = y.astype(jnp.float32) * jax.nn.silu(z.astype(jnp.float32))
    yg = yf.reshape(y.shape[:-1] + (M_GROUPS, M_D_INNER // M_GROUPS))
    yg = yg * lax.rsqrt(jnp.mean(yg * yg, axis=-1, keepdims=True) + M_NORM_EPS)
    return (yg.reshape(y.shape) * w.astype(jnp.float32)).astype(y.dtype)


def ssd_chunked_scan(x, dt, A, Bm, Cm):
    b, L, H, P = x.shape
    nc = L // M_CHUNK

    def to_chunks(t):
        t = t.reshape((b, nc, M_CHUNK) + t.shape[2:])
        return jnp.moveaxis(t, 1, 0)

    xg = to_chunks(x.astype(jnp.float32).reshape(b, L, M_GROUPS, M_HEADS_PER_GROUP, P))
    dtg = to_chunks(dt.reshape(b, L, M_GROUPS, M_HEADS_PER_GROUP))
    Bc = to_chunks(Bm.astype(jnp.float32))
    Cc = to_chunks(Cm.astype(jnp.float32))
    Ag = A.reshape(M_GROUPS, M_HEADS_PER_GROUP)
    causal = jnp.tril(jnp.ones((M_CHUNK, M_CHUNK), dtype=bool))[None, :, :, None, None]

    def step(state, inp):
        xc, dtc, Bcc, Ccc = inp
        acs = jnp.cumsum(dtc * Ag, axis=1)
        seg = acs[:, :, None] - acs[:, None, :]
        decay = jnp.exp(jnp.where(causal, seg, -jnp.inf))
        cb = jnp.einsum('btgn,bsgn->btsg', Ccc, Bcc)
        w = cb[..., None] * decay * dtc[:, None]
        y_diag = jnp.einsum('btsgj,bsgjp->btgjp', w, xc)
        y_off = jnp.einsum('btgn,bgjpn->btgjp', Ccc, state) * jnp.exp(acs)[..., None]
        to_end = jnp.exp(acs[:, -1:] - acs) * dtc
        new_state = (state * jnp.exp(acs[:, -1])[..., None, None]
                     + jnp.einsum('bsgj,bsgjp,bsgn->bgjpn', to_end, xc, Bcc))
        return new_state, y_diag + y_off

    state0 = jnp.zeros((b, M_GROUPS, M_HEADS_PER_GROUP, P, M_D_STATE), jnp.float32)
    _, y = lax.scan(step, state0, (xg, dtg, Bc, Cc))
    return jnp.moveaxis(y, 0, 1).reshape(b, L, H, P)


def mamba2_mixer(u, in_proj, conv_w, conv_b, dt_bias, a_log, d_skip, norm_w, out_proj):
    b, L, _ = u.shape
    zxbcdt = u @ in_proj
    z = zxbcdt[..., :M_D_INNER]
    xbc = zxbcdt[..., M_D_INNER:M_D_INNER + M_CONV_DIM]
    dt = zxbcdt[..., M_D_INNER + M_CONV_DIM:]
    xbc = lax.conv_general_dilated(
        xbc, conv_w[:, None, :].astype(xbc.dtype), window_strides=(1,),
        padding=((M_D_CONV - 1, 0),), dimension_numbers=('NWC', 'WIO', 'NWC'),
        feature_group_count=M_CONV_DIM) + conv_b
    xbc = jax.nn.silu(xbc)
    xs = xbc[..., :M_D_INNER]
    Bm = xbc[..., M_D_INNER:M_D_INNER + M_GROUPS * M_D_STATE].reshape(b, L, M_GROUPS, M_D_STATE)
    Cm = xbc[..., M_D_INNER + M_GROUPS * M_D_STATE:].reshape(b, L, M_GROUPS, M_D_STATE)
    dt = jax.nn.softplus(dt.astype(jnp.float32) + dt_bias.astype(jnp.float32))
    A = -jnp.exp(a_log.astype(jnp.float32))
    xh = xs.reshape(b, L, M_HEADS, M_HEAD_DIM)
    y = ssd_chunked_scan(xh, dt, A, Bm, Cm)
    y = y + xh.astype(jnp.float32) * d_skip.astype(jnp.float32)[:, None]
    y = y.reshape(b, L, M_D_INNER).astype(u.dtype)
    y = gated_group_rms_norm(y, z, norm_w)
    return y @ out_proj


def stick_breaking_attention(q, k, v):
    b, h, S, d = q.shape
    nblk = S // SB_BLOCK
    qb = jnp.moveaxis(q.reshape(b, h, nblk, SB_BLOCK, d), 2, 0)
    key_pos = jnp.arange(S)

    def block(args):
        q_blk, blk = args
        z = jnp.einsum('bhqd,bhkd->bhqk', q_blk, k).astype(jnp.float32)
        qpos = blk * SB_BLOCK + jnp.arange(SB_BLOCK)
        strict = key_pos[None, :] < qpos[:, None]
        log_1mb = jnp.where(strict, jax.nn.log_sigmoid(-z), 0.0)
        log_after = lax.cumsum(log_1mb, axis=3, reverse=True) - log_1mb
        a = jnp.where(strict, jnp.exp(jax.nn.log_sigmoid(z) + log_after), 0.0)
        return jnp.einsum('bhqk,bhkd->bhqd', a.astype(v.dtype), v)

    o = lax.map(block, (qb, jnp.arange(nblk)))
    return jnp.moveaxis(o, 0, 2).reshape(b, h, S, d)


def peer_layer(xn, w_q, sub_keys, u_tab, v_tab):
    b, S, D = xn.shape
    T = b * S
    xt = xn.reshape(T, D)
    q = (xt @ w_q).reshape(T, PEER_HEADS, PEER_KEY_DIM).astype(jnp.float32)
    s1 = jnp.einsum('thd,kd->thk', q[..., :PEER_HALF], sub_keys[0].astype(jnp.float32))
    s2 = jnp.einsum('thd,kd->thk', q[..., PEER_HALF:], sub_keys[1].astype(jnp.float32))
    v1, i1 = lax.top_k(s1, PEER_TOPK)
    v2, i2 = lax.top_k(s2, PEER_TOPK)
    cand = (v1[..., :, None] + v2[..., None, :]).reshape(T, PEER_HEADS, PEER_TOPK * PEER_TOPK)
    sc, ci = lax.top_k(cand, PEER_TOPK)
    e1 = jnp.take_along_axis(i1, ci // PEER_TOPK, axis=-1)
    e2 = jnp.take_along_axis(i2, ci % PEER_TOPK, axis=-1)
    idx = e1 * PEER_N_KEYS + e2
    gate = jax.nn.softmax(sc, axis=-1)
    nb = T // PEER_TOKEN_BLOCK

    def block(args):
        xc, ic, gc = args
        hid = jnp.einsum('chkd,cd->chk', u_tab[ic], xc).astype(jnp.float32)
        wgt = (gc * jax.nn.gelu(hid, approximate=False)).astype(v_tab.dtype)
        return jnp.einsum('chk,chkd->cd', wgt, v_tab[ic])

    out = lax.map(block, (xt.reshape(nb, PEER_TOKEN_BLOCK, D),
                          idx.reshape(nb, PEER_TOKEN_BLOCK, PEER_HEADS, PEER_TOPK),
                          gate.reshape(nb, PEER_TOKEN_BLOCK, PEER_HEADS, PEER_TOPK)))
    return out.reshape(b, S, D).astype(xn.dtype)


def setup_inputs(seed: int = 0) -> dict:
    key = jax.random.key(seed)
    ks = jax.random.split(key, 24)
    f32 = jnp.float32
    nrm = lambda k, shape, scale: jax.random.normal(k, shape, f32) * scale
    x = jax.random.normal(ks[0], (BATCH, SEQ, D_MODEL), f32)
    norm_mix = 1.0 + nrm(ks[1], (DEPTH, D_MODEL), 0.02)
    norm_ffn = 1.0 + nrm(ks[2], (DEPTH, D_MODEL), 0.02)
    norm_final = 1.0 + nrm(ks[3], (D_MODEL,), 0.02)
    m_in_proj = nrm(ks[4], (N_A_LAYERS, D_MODEL, M_IN_PROJ), D_MODEL ** -0.5)
    m_conv_w = nrm(ks[5], (N_A_LAYERS, M_D_CONV, M_CONV_DIM), M_D_CONV ** -0.5)
    m_conv_b = nrm(ks[6], (N_A_LAYERS, M_CONV_DIM), 0.02)
    dt0 = jnp.exp(jax.random.uniform(ks[7], (N_A_LAYERS, M_HEADS), f32,
                                     math.log(1e-3), math.log(1e-1)))
    m_dt_bias = dt0 + jnp.log(-jnp.expm1(-dt0))
    m_a_log = jnp.log(jax.random.uniform(ks[8], (N_A_LAYERS, M_HEADS), f32, 1.0, 16.0))
    m_d = 1.0 + nrm(ks[9], (N_A_LAYERS, M_HEADS), 0.02)
    m_norm = 1.0 + nrm(ks[10], (N_A_LAYERS, M_D_INNER), 0.02)
    m_out_proj = nrm(ks[11], (N_A_LAYERS, M_D_INNER, D_MODEL), M_D_INNER ** -0.5)
    kv_norm = 1.0 + nrm(ks[12], (D_MODEL,), 0.02)
    w_kv = nrm(ks[13], (D_MODEL, 2 * SB_WIDTH), D_MODEL ** -0.5)
    sb_w_q = nrm(ks[14], (N_B_LAYERS, D_MODEL, SB_WIDTH), D_MODEL ** -0.5)
    sb_w_o = nrm(ks[15], (N_B_LAYERS, SB_WIDTH, D_MODEL), SB_WIDTH ** -0.5)
    peer_w_q = nrm(ks[16], (DEPTH, D_MODEL, PEER_HEADS * PEER_KEY_DIM), D_MODEL ** -0.5)
    peer_sub_keys = nrm(ks[17], (DEPTH, 2, PEER_N_KEYS, PEER_HALF), PEER_HALF ** -0.5)
    peer_u = nrm(ks[18], (DEPTH, PEER_EXPERTS, D_MODEL), D_MODEL ** -0.5)
    peer_v = nrm(ks[19], (DEPTH, PEER_EXPERTS, D_MODEL), PEER_HEADS ** -0.5)
    return {"x": x, "norm_mix": norm_mix, "norm_ffn": norm_ffn, "norm_final": norm_final,
            "m_in_proj": m_in_proj, "m_conv_w": m_conv_w, "m_conv_b": m_conv_b,
            "m_dt_bias": m_dt_bias, "m_a_log": m_a_log, "m_d": m_d, "m_norm": m_norm,
            "m_out_proj": m_out_proj, "kv_norm": kv_norm, "w_kv": w_kv,
            "sb_w_q": sb_w_q, "sb_w_o": sb_w_o, "peer_w_q": peer_w_q,
            "peer_sub_keys": peer_sub_keys, "peer_u": peer_u, "peer_v": peer_v}


def reference(x, norm_mix, norm_ffn, norm_final, m_in_proj, m_conv_w, m_conv_b, m_dt_bias,
              m_a_log, m_d, m_norm, m_out_proj, kv_norm, w_kv, sb_w_q, sb_w_o, peer_w_q,
              peer_sub_keys, peer_u, peer_v):
    b, S, _ = x.shape
    h = x
    k = None
    v = None
    for layer in range(DEPTH):
        if layer < N_A_LAYERS:
            i = layer
            h = h + mamba2_mixer(rms_norm(h, norm_mix[layer]), m_in_proj[i], m_conv_w[i],
                                 m_conv_b[i], m_dt_bias[i], m_a_log[i], m_d[i], m_norm[i],
                                 m_out_proj[i])
        else:
            i = layer - N_A_LAYERS
            if layer == N_A_LAYERS:
                kv = rms_norm(h, kv_norm) @ w_kv
                k = kv[..., :SB_WIDTH].reshape(b, S, SB_HEADS, SB_HEAD_DIM).transpose(0, 2, 1, 3)
                v = kv[..., SB_WIDTH:].reshape(b, S, SB_HEADS, SB_HEAD_DIM).transpose(0, 2, 1, 3)
            q = (rms_norm(h, norm_mix[layer]) @ sb_w_q[i]).reshape(
                b, S, SB_HEADS, SB_HEAD_DIM).transpose(0, 2, 1, 3) * SB_SCALE
            o = stick_breaking_attention(q, k, v)
            h = h + o.transpose(0, 2, 1, 3).reshape(b, S, SB_WIDTH) @ sb_w_o[i]
        h = h + peer_layer(rms_norm(h, norm_ffn[layer]), peer_w_q[layer], peer_sub_keys[layer],
                           peer_u[layer], peer_v[layer])
    return rms_norm(h, norm_final)
```

```python
import functools
import math

import jax
import jax.numpy as jnp
from jax import lax
from jax.experimental import pallas as pl
from jax.experimental.pallas import tpu as pltpu

F32 = jnp.float32
BF16 = jnp.bfloat16
HIGHEST = lax.Precision.HIGHEST

NORM_EPS = 1e-6
LANES = 128
SUBLANES = 8

M_HEAD_DIM = 64
M_D_STATE = 128
M_GROUPS = 8
M_D_CONV = 4
M_CHUNK = 128
M_NORM_EPS = 1e-5

SB_HEAD_DIM = 128
SB_BLOCK = 128

PEER_HEADS = 8
PEER_N_KEYS = 128
PEER_TOPK = 16
PEER_HALF = 128
PEER_SLOTS = PEER_HEADS * PEER_TOPK

VMEM_LIMIT = 48 * 1024 * 1024


def _params(*sem):
    return pltpu.CompilerParams(dimension_semantics=sem, vmem_limit_bytes=VMEM_LIMIT)


def _rmsnorm_body(x_ref, g_ref, *o_refs, eps):
    x = x_ref[...]
    y = x * lax.rsqrt(jnp.mean(x * x, axis=-1, keepdims=True) + eps) * g_ref[...]
    for o_ref in o_refs:
        o_ref[...] = y.astype(o_ref.dtype)


def rmsnorm(x, g, out_dtypes, tm=256):
    t, d = x.shape
    outs = pl.pallas_call(
        functools.partial(_rmsnorm_body, eps=NORM_EPS),
        grid=(t // tm,),
        in_specs=[pl.BlockSpec((tm, d), lambda i: (i, 0)), pl.BlockSpec((1, d), lambda i: (0, 0))],
        out_specs=[pl.BlockSpec((tm, d), lambda i: (i, 0)) for _ in out_dtypes],
        out_shape=[jax.ShapeDtypeStruct((t, d), dt) for dt in out_dtypes],
        compiler_params=_params("arbitrary"),
        name="rmsnorm",
    )(x, g.reshape(1, d))
    return outs


def _mm_body(a_ref, w_ref, o_ref, *, precision):
    o_ref[...] = jnp.dot(a_ref[...], w_ref[...], preferred_element_type=F32,
                         precision=precision).astype(o_ref.dtype)


def _mm_res_body(a_ref, w_ref, r_ref, o_ref, *, precision):
    o_ref[...] = r_ref[...] + jnp.dot(a_ref[...], w_ref[...], preferred_element_type=F32,
                                      precision=precision)


def matmul(a, w, residual=None, out_dtype=F32, precision=None, tm=1024, tn=512, name="matmul"):
    m, k = a.shape
    n = w.shape[1]
    tn = min(tn, n)
    in_specs = [pl.BlockSpec((tm, k), lambda i, j: (i, 0)), pl.BlockSpec((k, tn), lambda i, j: (0, j))]
    args = [a, w]
    if residual is None:
        body = functools.partial(_mm_body, precision=precision)
    else:
        body = functools.partial(_mm_res_body, precision=precision)
        in_specs.append(pl.BlockSpec((tm, tn), lambda i, j: (i, j)))
        args.append(residual)
    return pl.pallas_call(
        body,
        grid=(m // tm, n // tn),
        in_specs=in_specs,
        out_specs=pl.BlockSpec((tm, tn), lambda i, j: (i, j)),
        out_shape=jax.ShapeDtypeStruct((m, n), out_dtype),
        compiler_params=_params("arbitrary", "arbitrary"),
        name=name,
    )(*args)


def _softplus(x):
    return jnp.maximum(x, 0.0) + jnp.log1p(jnp.exp(-jnp.abs(x)))


def _silu(x):
    return x * jax.nn.sigmoid(x)


def _causal_conv_silu(raw_ref, prev_ref, w_ref, bias_ref, out_ref):
    cur = raw_ref[...]
    prev8 = prev_ref[...]
    w = w_ref[...]
    bias = bias_ref[...]
    top = cur[0:SUBLANES, :]
    row8 = lax.broadcasted_iota(jnp.int32, top.shape, 0)
    acc = cur * w[M_D_CONV - 1:M_D_CONV, :]
    acc_top = top * w[M_D_CONV - 1:M_D_CONV, :]
    for d in range(1, M_D_CONV):
        wd = w[M_D_CONV - 1 - d:M_D_CONV - d, :]
        shifted = pltpu.roll(cur, d, 0)
        acc = acc + shifted * wd
        shifted_top = jnp.where(row8 < d, pltpu.roll(prev8, d, 0), shifted[0:SUBLANES, :])
        acc_top = acc_top + shifted_top * wd
    out_ref[...] = _silu(acc + bias)
    out_ref[0:SUBLANES, :] = _silu(acc_top + bias)
    prev_ref[...] = cur[M_CHUNK - SUBLANES:M_CHUNK, :]


def _mamba_body(z_ref, x_ref, b_ref, c_ref, dtr_ref, cwx_ref, cwb_ref, cwc_ref, cbx_ref, cbb_ref, cbc_ref,
                dtb_ref, alog_ref, dexp_ref, nw_ref, e_ref, o_ref,
                state_ref, px_ref, pb_ref, pc_ref, xc_ref, bc_ref, cc_ref, y_ref):
    @pl.when(pl.program_id(1) == 0)
    def _():
        state_ref[...] = jnp.zeros_like(state_ref)
        px_ref[...] = jnp.zeros_like(px_ref)
        pb_ref[...] = jnp.zeros_like(pb_ref)
        pc_ref[...] = jnp.zeros_like(pc_ref)

    _causal_conv_silu(x_ref, px_ref, cwx_ref, cbx_ref, xc_ref)
    _causal_conv_silu(b_ref, pb_ref, cwb_ref, cbb_ref, bc_ref)
    _causal_conv_silu(c_ref, pc_ref, cwc_ref, cbc_ref, cc_ref)

    L = M_CHUNK
    row = lax.broadcasted_iota(jnp.int32, (L, L), 0)
    col = lax.broadcasted_iota(jnp.int32, (L, L), 1)
    causal = col <= row
    ltri = causal.astype(F32)

    dt = _softplus(dtr_ref[...] + dtb_ref[...])
    dta = dt * (-jnp.exp(alog_ref[...]))
    acs = jnp.dot(ltri, dta, preferred_element_type=F32, precision=HIGHEST)
    dt_t = dt.T
    acs_t = acs.T
    e = e_ref[...]
    dt_exp = jnp.dot(dt, e, preferred_element_type=F32, precision=HIGHEST)
    acs_exp = jnp.dot(acs, e, preferred_element_type=F32, precision=HIGHEST)

    gw = M_HEAD_DIM * 8
    first_half = col < M_HEAD_DIM
    for g in range(M_GROUPS):
        bg = bc_ref[:, g * M_D_STATE:(g + 1) * M_D_STATE].astype(BF16)
        cg = cc_ref[:, g * M_D_STATE:(g + 1) * M_D_STATE].astype(BF16)
        xg = xc_ref[:, g * gw:(g + 1) * gw]
        ae = acs_exp[:, g * gw:(g + 1) * gw]
        de = dt_exp[:, g * gw:(g + 1) * gw]
        cb = lax.dot_general(cg, bg, (((1,), (1,)), ((), ())), preferred_element_type=F32)
        sg = state_ref[g]
        y_off = jnp.dot(cg, sg.astype(BF16), preferred_element_type=F32) * jnp.exp(ae)
        a_last = ae[L - 1:L, :]
        to_end = jnp.exp(a_last - ae) * de
        xw = (xg * to_end).astype(BF16)
        upd = lax.dot_general(bg, xw, (((0,), (0,)), ((), ())), preferred_element_type=F32)
        state_ref[g] = sg * jnp.exp(a_last) + upd
        for jp in range(4):
            xpair = xg[:, jp * LANES:(jp + 1) * LANES]
            ys = jnp.zeros((L, LANES), F32)
            for half in range(2):
                h = g * 8 + jp * 2 + half
                a_col = jnp.sum(jnp.where(col == h, acs, 0.0), axis=1, keepdims=True)
                a_row = acs_t[h:h + 1, :]
                d_row = dt_t[h:h + 1, :]
                decay = jnp.exp(jnp.where(causal, a_col - a_row, -jnp.inf))
                w = (cb * decay * d_row).astype(BF16)
                keep = first_half if half == 0 else jnp.logical_not(first_half)
                xm = jnp.where(keep, xpair, 0.0).astype(BF16)
                ys = ys + jnp.dot(w, xm, preferred_element_type=F32)
            lo = g * gw + jp * LANES
            y_ref[:, lo:lo + LANES] = (ys + y_off[:, jp * LANES:(jp + 1) * LANES]
                                       + xpair * dexp_ref[:, lo:lo + LANES])

    for g in range(M_GROUPS):
        sl = slice(g * gw, (g + 1) * gw)
        yf = y_ref[:, sl] * _silu(z_ref[:, sl])
        ms = jnp.mean(yf * yf, axis=1, keepdims=True)
        o_ref[:, sl] = (yf * lax.rsqrt(ms + M_NORM_EPS) * nw_ref[:, sl]).astype(o_ref.dtype)


def mamba_ssd(zx, dtr, conv_w, conv_b, dt_bias, a_log, d_skip, norm_w, batch, seq):
    t = zx.shape[0]
    d_inner = norm_w.shape[0]
    n_heads = d_inner // M_HEAD_DIM
    bc_w = M_GROUPS * M_D_STATE
    nc = seq // M_CHUNK
    assert d_inner % 2048 == 0 and bc_w == 1024 and zx.shape[1] == 2 * d_inner + 2 * bc_w

    pad_h = LANES - n_heads
    dtb = jnp.pad(dt_bias, (0, pad_h)).reshape(1, LANES)
    alog = jnp.pad(a_log, (0, pad_h)).reshape(1, LANES)
    dexp = jnp.repeat(d_skip, M_HEAD_DIM).reshape(1, d_inner)
    expand = (jnp.arange(LANES)[:, None] == (jnp.arange(d_inner)[None, :] // M_HEAD_DIM)).astype(F32)
    cwx, cwb, cwc = conv_w[:, :d_inner], conv_w[:, d_inner:d_inner + bc_w], conv_w[:, d_inner + bc_w:]
    cb2 = conv_b.reshape(1, -1)
    cbx, cbb, cbc = cb2[:, :d_inner], cb2[:, d_inner:d_inner + bc_w], cb2[:, d_inner + bc_w:]

    row_map = lambda b, c: (b * nc + c, 0)
    const = lambda b, c: (0, 0)
    xblk = d_inner // bc_w
    in_specs = [
        pl.BlockSpec((M_CHUNK, d_inner), row_map),
        pl.BlockSpec((M_CHUNK, d_inner), lambda b, c: (b * nc + c, 1)),
        pl.BlockSpec((M_CHUNK, bc_w), lambda b, c: (b * nc + c, 2 * xblk)),
        pl.BlockSpec((M_CHUNK, bc_w), lambda b, c: (b * nc + c, 2 * xblk + 1)),
        pl.BlockSpec((M_CHUNK, LANES), row_map),
        pl.BlockSpec((M_D_CONV, d_inner), const), pl.BlockSpec((M_D_CONV, bc_w), const),
        pl.BlockSpec((M_D_CONV, bc_w), const),
        pl.BlockSpec((1, d_inner), const), pl.BlockSpec((1, bc_w), const), pl.BlockSpec((1, bc_w), const),
        pl.BlockSpec((1, LANES), const), pl.BlockSpec((1, LANES), const),
        pl.BlockSpec((1, d_inner), const), pl.BlockSpec((1, d_inner), const),
        pl.BlockSpec((LANES, d_inner), const),
    ]
    return pl.pallas_call(
        _mamba_body,
        grid=(batch, nc),
        in_specs=in_specs,
        out_specs=pl.BlockSpec((M_CHUNK, d_inner), row_map),
        out_shape=jax.ShapeDtypeStruct((t, d_inner), BF16),
        scratch_shapes=[
            pltpu.VMEM((M_GROUPS, M_D_STATE, M_HEAD_DIM * 8), F32),
            pltpu.VMEM((SUBLANES, d_inner), F32), pltpu.VMEM((SUBLANES, bc_w), F32),
            pltpu.VMEM((SUBLANES, bc_w), F32),
            pltpu.VMEM((M_CHUNK, d_inner), F32), pltpu.VMEM((M_CHUNK, bc_w), F32),
            pltpu.VMEM((M_CHUNK, bc_w), F32),
            pltpu.VMEM((M_CHUNK, d_inner), F32),
        ],
        compiler_params=_params("arbitrary", "arbitrary"),
        name="mamba_ssd",
    )(zx, zx, zx, zx, dtr, cwx, cwb, cwc, cbx, cbb, cbc, dtb, alog, dexp, norm_w.reshape(1, d_inner), expand)


def _attn_body(q_ref, k_ref, v_ref, o_ref, *, scale):
    qb = pl.program_id(2)
    blk = SB_BLOCK
    q = (q_ref[...] * scale).astype(BF16)
    row = lax.broadcasted_iota(jnp.int32, (blk, blk), 0)
    col = lax.broadcasted_iota(jnp.int32, (blk, blk), 1)
    later = (row > col).astype(F32)

    def step(i, carry):
        acc, tail = carry
        kb = qb - i
        start = pl.multiple_of(kb * blk, blk)
        k = k_ref[pl.ds(start, blk), :].astype(BF16)
        v = v_ref[pl.ds(start, blk), :].astype(BF16)
        z = lax.dot_general(q, k, (((1,), (1,)), ((), ())), preferred_element_type=F32)
        lp = jnp.log1p(jnp.exp(-jnp.abs(z)))
        log_1mb = -(jnp.maximum(z, 0.0) + lp)
        log_b = z + log_1mb
        strict = (kb * blk + col) < (qb * blk + row)
        log_1mb = jnp.where(strict, log_1mb, 0.0)
        log_after = jnp.dot(log_1mb, later, preferred_element_type=F32, precision=HIGHEST) + tail
        a = jnp.where(strict, jnp.exp(log_b + log_after), 0.0)
        acc = acc + jnp.dot(a.astype(BF16), v, preferred_element_type=F32)
        tail = tail + jnp.sum(log_1mb, axis=1, keepdims=True)
        return acc, tail

    acc, _ = lax.fori_loop(0, qb + 1, step, (jnp.zeros((blk, SB_HEAD_DIM), F32), jnp.zeros((blk, 1), F32)))
    o_ref[...] = acc.astype(o_ref.dtype)


def stick_breaking_attention(q, kv, batch, seq):
    t, width = q.shape
    n_heads = width // SB_HEAD_DIM
    nq = seq // SB_BLOCK
    return pl.pallas_call(
        functools.partial(_attn_body, scale=1.0 / math.sqrt(SB_HEAD_DIM)),
        grid=(batch, n_heads, nq),
        in_specs=[
            pl.BlockSpec((SB_BLOCK, SB_HEAD_DIM), lambda b, h, i: (b * nq + i, h)),
            pl.BlockSpec((seq, SB_HEAD_DIM), lambda b, h, i: (b, h)),
            pl.BlockSpec((seq, SB_HEAD_DIM), lambda b, h, i: (b, n_heads + h)),
        ],
        out_specs=pl.BlockSpec((SB_BLOCK, SB_HEAD_DIM), lambda b, h, i: (b * nq + i, h)),
        out_shape=jax.ShapeDtypeStruct((t, width), BF16),
        compiler_params=_params("arbitrary", "arbitrary", "arbitrary"),
        name="stick_breaking_attention",
    )(q, kv, kv)


def _top16_rows(s, n_rows):
    tokens = s.shape[1]
    row = lax.broadcasted_iota(jnp.int32, s.shape, 0)
    slot = lax.broadcasted_iota(jnp.int32, (PEER_TOPK, tokens), 0)
    vals = jnp.zeros((PEER_TOPK, tokens), F32)
    idxs = jnp.zeros((PEER_TOPK, tokens), jnp.int32)
    for r in range(PEER_TOPK):
        m = jnp.max(s, axis=0, keepdims=True)
        first = jnp.min(jnp.where(s == m, row, n_rows), axis=0, keepdims=True)
        s = jnp.where(row == first, -jnp.inf, s)
        vals = jnp.where(slot == r, m, vals)
        idxs = jnp.where(slot == r, first, idxs)
    return vals, idxs


def _peer_topk_body(q_ref, k1_ref, k2_ref, idx_ref, gate_ref):
    k1 = k1_ref[...]
    k2 = k2_ref[...]
    nt = (((1,), (1,)), ((), ()))
    for h in range(PEER_HEADS):
        base = h * 2 * PEER_HALF
        q1 = q_ref[:, base:base + PEER_HALF]
        q2 = q_ref[:, base + PEER_HALF:base + 2 * PEER_HALF]
        s1 = lax.dot_general(k1, q1, nt, preferred_element_type=F32, precision=HIGHEST)
        s2 = lax.dot_general(k2, q2, nt, preferred_element_type=F32, precision=HIGHEST)
        v1, i1 = _top16_rows(s1, PEER_N_KEYS)
        v2, i2 = _top16_rows(s2, PEER_N_KEYS)
        cand = jnp.concatenate([v1[a:a + 1, :] + v2 for a in range(PEER_TOPK)], axis=0)
        sc, ci = _top16_rows(cand, PEER_TOPK * PEER_TOPK)
        ca = ci // PEER_TOPK
        cbi = ci % PEER_TOPK
        e1 = jnp.zeros_like(ci)
        e2 = jnp.zeros_like(ci)
        for a in range(PEER_TOPK):
            e1 = jnp.where(ca == a, i1[a:a + 1, :], e1)
            e2 = jnp.where(cbi == a, i2[a:a + 1, :], e2)
        p = jnp.exp(sc - jnp.max(sc, axis=0, keepdims=True))
        gate = p / jnp.sum(p, axis=0, keepdims=True)
        idx_ref[h * PEER_TOPK:(h + 1) * PEER_TOPK, :] = e1 * PEER_N_KEYS + e2
        gate_ref[h * PEER_TOPK:(h + 1) * PEER_TOPK, :] = gate


def peer_topk(q, sub_keys, tt=256):
    t, width = q.shape
    assert width == PEER_HEADS * 2 * PEER_HALF
    return pl.pallas_call(
        _peer_topk_body,
        grid=(t // tt,),
        in_specs=[
            pl.BlockSpec((tt, width), lambda i: (i, 0)),
            pl.BlockSpec((PEER_N_KEYS, PEER_HALF), lambda i: (0, 0)),
            pl.BlockSpec((PEER_N_KEYS, PEER_HALF), lambda i: (0, 0)),
        ],
        out_specs=[pl.BlockSpec((PEER_SLOTS, tt), lambda i: (0, i)),
                   pl.BlockSpec((PEER_SLOTS, tt), lambda i: (0, i))],
        out_shape=[jax.ShapeDtypeStruct((PEER_SLOTS, t), jnp.int32),
                   jax.ShapeDtypeStruct((PEER_SLOTS, t), F32)],
        compiler_params=_params("arbitrary"),
        name="peer_topk",
    )(q, sub_keys[0], sub_keys[1])


def _gelu(x):
    return 0.5 * x * (1.0 + lax.erf(x * (1.0 / math.sqrt(2.0))))


def _peer_mix_body(idx_hbm, x_ref, gate_ref, h_ref, u_hbm, v_hbm, o_ref,
                   idx_smem, ubuf, vbuf, mix_ref, idx_sem, row_sem, *, tb):
    step = pl.program_id(0)
    n_idx = tb * PEER_SLOTS
    off = pl.multiple_of(step * n_idx, n_idx)
    idx_copy = pltpu.make_async_copy(idx_hbm.at[pl.ds(off, n_idx)], idx_smem, idx_sem)
    idx_copy.start()
    idx_copy.wait()

    def row_copies(j, slot, k):
        e = idx_smem[j * PEER_SLOTS + k]
        cu = pltpu.make_async_copy(u_hbm.at[pl.ds(e, 1), :], ubuf.at[slot, pl.ds(k, 1), :], row_sem.at[0, slot])
        cv = pltpu.make_async_copy(v_hbm.at[pl.ds(e, 1), :], vbuf.at[slot, pl.ds(k, 1), :], row_sem.at[1, slot])
        return cu, cv

    def issue(j, slot):
        def body(k, carry):
            cu, cv = row_copies(j, slot, k)
            cu.start()
            cv.start()
            return carry
        lax.fori_loop(0, PEER_SLOTS, body, 0, unroll=8)

    def wait(slot):
        pltpu.make_async_copy(u_hbm.at[pl.ds(0, PEER_SLOTS), :], ubuf.at[slot], row_sem.at[0, slot]).wait()
        pltpu.make_async_copy(v_hbm.at[pl.ds(0, PEER_SLOTS), :], vbuf.at[slot], row_sem.at[1, slot]).wait()

    d = x_ref.shape[1]
    n_chunks = d // LANES
    row = lax.broadcasted_iota(jnp.int32, (PEER_SLOTS, LANES), 0)
    col = lax.broadcasted_iota(jnp.int32, (PEER_SLOTS, LANES), 1)
    eye = row == col

    n_groups = tb // SUBLANES
    issue(0, 0)

    def group(g, carry):
        base = pl.multiple_of(g * SUBLANES, SUBLANES)
        rows = pl.ds(base, SUBLANES)
        gate8 = gate_ref[rows, :]
        for r in range(SUBLANES):
            slot = r % 2
            if r + 1 < SUBLANES:
                issue(base + r + 1, 1 - slot)
            else:
                @pl.when(g + 1 < n_groups)
                def _():
                    issue(base + SUBLANES, 1 - slot)
            wait(slot)
            u = ubuf.at[slot]
            v = vbuf.at[slot]
            acc = jnp.zeros((PEER_SLOTS, LANES), F32)
            for c in range(n_chunks):
                sl = slice(c * LANES, (c + 1) * LANES)
                acc = acc + u[:, sl] * x_ref[rows, sl][r:r + 1, :]
            hid = jnp.sum(acc, axis=1, keepdims=True)
            gate_row = jnp.broadcast_to(gate8[r:r + 1, :], (PEER_SLOTS, LANES))
            gate = jnp.sum(jnp.where(eye, gate_row, 0.0), axis=1, keepdims=True)
            w = jnp.broadcast_to(gate * _gelu(hid), (PEER_SLOTS, LANES))
            for c in range(n_chunks):
                sl = slice(c * LANES, (c + 1) * LANES)
                mixed = jnp.sum(v[:, sl] * w, axis=0, keepdims=True)
                mix_ref[r:r + 1, sl] = mixed
        o_ref[rows, :] = h_ref[rows, :] + mix_ref[...]
        return carry

    lax.fori_loop(0, n_groups, group, 0)


def peer_mix(xn, idx, gate, h, u_tab, v_tab, tb=64):
    t, d = xn.shape
    assert idx.shape == (t, PEER_SLOTS) and gate.shape == (t, PEER_SLOTS)
    blk = lambda i: (i, 0)
    return pl.pallas_call(
        functools.partial(_peer_mix_body, tb=tb),
        grid=(t // tb,),
        in_specs=[
            pl.BlockSpec(memory_space=pl.ANY),
            pl.BlockSpec((tb, d), blk),
            pl.BlockSpec((tb, PEER_SLOTS), blk),
            pl.BlockSpec((tb, d), blk),
            pl.BlockSpec(memory_space=pl.ANY),
            pl.BlockSpec(memory_space=pl.ANY),
        ],
        out_specs=pl.BlockSpec((tb, d), blk),
        out_shape=jax.ShapeDtypeStruct((t, d), F32),
        scratch_shapes=[
            pltpu.SMEM((tb * PEER_SLOTS,), jnp.int32),
            pltpu.VMEM((2, PEER_SLOTS, d), F32),
            pltpu.VMEM((2, PEER_SLOTS, d), F32),
            pltpu.VMEM((SUBLANES, d), F32),
            pltpu.SemaphoreType.DMA(()),
            pltpu.SemaphoreType.DMA((2, 2)),
        ],
        compiler_params=_params("arbitrary"),
        name="peer_mix",
    )(idx.reshape(-1), xn, gate, h, u_tab, v_tab)


def peer_layer(h, norm_g, w_q, sub_keys, u_tab, v_tab):
    xn, = rmsnorm(h, norm_g, [F32])
    q = matmul(xn, w_q, precision=HIGHEST, tm=512, name="peer_query")
    idx_t, gate_t = peer_topk(q, sub_keys)
    return peer_mix(xn, idx_t.T, gate_t.T, h, u_tab, v_tab)


def kernel(x, norm_mix, norm_ffn, norm_final, m_in_proj, m_conv_w, m_conv_b, m_dt_bias, m_a_log, m_d, m_norm,
           m_out_proj, kv_norm, w_kv, sb_w_q, sb_w_o, peer_w_q, peer_sub_keys, peer_u, peer_v):
    batch, seq, d_model = x.shape
    t = batch * seq
    h = x.reshape(t, d_model)

    d_inner = m_norm.shape[1]
    n_heads = m_dt_bias.shape[1]
    proj_w = m_in_proj.shape[2] - n_heads
    u, = rmsnorm(h, norm_mix[0], [BF16])
    w_in = m_in_proj[0]
    zx = matmul(u, w_in[:, :proj_w].astype(BF16), tn=1024, name="in_proj")
    w_dt = jnp.pad(w_in[:, proj_w:], ((0, 0), (0, LANES - n_heads))).astype(BF16)
    dtr = matmul(u, w_dt, name="in_proj_dt")
    y = mamba_ssd(zx, dtr, m_conv_w[0], m_conv_b[0], m_dt_bias[0], m_a_log[0], m_d[0], m_norm[0], batch, seq)
    h = matmul(y, m_out_proj[0].astype(BF16), residual=h, name="out_proj")
    h = peer_layer(h, norm_ffn[0], peer_w_q[0], peer_sub_keys[0], peer_u[0], peer_v[0])

    kvn, qn = rmsnorm(h, kv_norm, [BF16])[0], rmsnorm(h, norm_mix[1], [BF16])[0]
    kv = matmul(kvn, w_kv.astype(BF16), name="kv_proj")
    q = matmul(qn, sb_w_q[0].astype(BF16), name="q_proj")
    o = stick_breaking_attention(q, kv, batch, seq)
    h = matmul(o, sb_w_o[0].astype(BF16), residual=h, name="o_proj")
    h = peer_layer(h, norm_ffn[1], peer_w_q[1], peer_sub_keys[1], peer_u[1], peer_v[1])

    out, = rmsnorm(h, norm_final, [F32])
    return out.reshape(batch, seq, d_model)
```

```python
import functools
import math

import jax
import jax.numpy as jnp
from jax import lax
from jax.experimental import pallas as pl
from jax.experimental.pallas import tpu as pltpu

F32 = jnp.float32
BF16 = jnp.bfloat16
U32 = jnp.uint32
HIGHEST = lax.Precision.HIGHEST

NORM_EPS = 1e-6
LANES = 128
SUBLANES = 8

M_HEAD_DIM = 64
M_D_STATE = 128
M_GROUPS = 8
M_D_CONV = 4
M_CHUNK = 128
M_NORM_EPS = 1e-5

SB_HEAD_DIM = 128
SB_BLOCK = 128
SB_Q_TILE = 512

PEER_HEADS = 8
PEER_N_KEYS = 128
PEER_TOPK = 16
PEER_HALF = 128
PEER_SLOTS = PEER_HEADS * PEER_TOPK
PEER_BUFS = 8
PEER_AHEAD = 6

VMEM_LIMIT = 48 * 1024 * 1024


def _params(*sem):
    return pltpu.CompilerParams(dimension_semantics=sem, vmem_limit_bytes=VMEM_LIMIT)


def _rmsnorm_body(x_ref, g_ref, *o_refs, eps):
    x = x_ref[...]
    y = x * lax.rsqrt(jnp.mean(x * x, axis=-1, keepdims=True) + eps) * g_ref[...]
    for o_ref in o_refs:
        o_ref[...] = y.astype(o_ref.dtype)


def rmsnorm(x, g, out_dtypes, tm=256):
    t, d = x.shape
    outs = pl.pallas_call(
        functools.partial(_rmsnorm_body, eps=NORM_EPS),
        grid=(t // tm,),
        in_specs=[pl.BlockSpec((tm, d), lambda i: (i, 0)), pl.BlockSpec((1, d), lambda i: (0, 0))],
        out_specs=[pl.BlockSpec((tm, d), lambda i: (i, 0)) for _ in out_dtypes],
        out_shape=[jax.ShapeDtypeStruct((t, d), dt) for dt in out_dtypes],
        compiler_params=_params("arbitrary"),
        name="rmsnorm",
    )(x, g.reshape(1, d))
    return outs


def _mm_body(a_ref, w_ref, o_ref, *, precision):
    o_ref[...] = jnp.dot(a_ref[...], w_ref[...], preferred_element_type=F32,
                         precision=precision).astype(o_ref.dtype)


def _mm_res_body(a_ref, w_ref, r_ref, o_ref, *, precision):
    o_ref[...] = r_ref[...] + jnp.dot(a_ref[...], w_ref[...], preferred_element_type=F32,
                                      precision=precision)


def matmul(a, w, residual=None, out_dtype=F32, precision=None, tm=1024, tn=512, name="matmul"):
    m, k = a.shape
    n = w.shape[1]
    tn = min(tn, n)
    in_specs = [pl.BlockSpec((tm, k), lambda i, j: (i, 0)), pl.BlockSpec((k, tn), lambda i, j: (0, j))]
    args = [a, w]
    if residual is None:
        body = functools.partial(_mm_body, precision=precision)
    else:
        body = functools.partial(_mm_res_body, precision=precision)
        in_specs.append(pl.BlockSpec((tm, tn), lambda i, j: (i, j)))
        args.append(residual)
    return pl.pallas_call(
        body,
        grid=(m // tm, n // tn),
        in_specs=in_specs,
        out_specs=pl.BlockSpec((tm, tn), lambda i, j: (i, j)),
        out_shape=jax.ShapeDtypeStruct((m, n), out_dtype),
        compiler_params=_params("arbitrary", "arbitrary"),
        name=name,
    )(*args)


def _softplus(x):
    return jnp.maximum(x, 0.0) + jnp.log1p(jnp.exp(-jnp.abs(x)))


def _silu(x):
    return x * jax.nn.sigmoid(x)


def _causal_conv_silu(raw_ref, prev_ref, w_ref, bias_ref, out_ref):
    cur = raw_ref[...]
    prev8 = prev_ref[...]
    w = w_ref[...]
    bias = bias_ref[...]
    top = cur[0:SUBLANES, :]
    row8 = lax.broadcasted_iota(jnp.int32, top.shape, 0)
    acc = cur * w[M_D_CONV - 1:M_D_CONV, :]
    acc_top = top * w[M_D_CONV - 1:M_D_CONV, :]
    for d in range(1, M_D_CONV):
        wd = w[M_D_CONV - 1 - d:M_D_CONV - d, :]
        shifted = pltpu.roll(cur, d, 0)
        acc = acc + shifted * wd
        shifted_top = jnp.where(row8 < d, pltpu.roll(prev8, d, 0), shifted[0:SUBLANES, :])
        acc_top = acc_top + shifted_top * wd
    out_ref[...] = _silu(acc + bias)
    out_ref[0:SUBLANES, :] = _silu(acc_top + bias)
    prev_ref[...] = cur[M_CHUNK - SUBLANES:M_CHUNK, :]


def _mamba_body(z_ref, x_ref, b_ref, c_ref, dtr_ref, cwx_ref, cwb_ref, cwc_ref, cbx_ref, cbb_ref, cbc_ref,
                dtb_ref, alog_ref, dexp_ref, nw_ref, e_ref, o_ref,
                state_ref, px_ref, pb_ref, pc_ref, xc_ref, bc_ref, cc_ref, y_ref):
    @pl.when(pl.program_id(1) == 0)
    def _():
        state_ref[...] = jnp.zeros_like(state_ref)
        px_ref[...] = jnp.zeros_like(px_ref)
        pb_ref[...] = jnp.zeros_like(pb_ref)
        pc_ref[...] = jnp.zeros_like(pc_ref)

    _causal_conv_silu(x_ref, px_ref, cwx_ref, cbx_ref, xc_ref)
    _causal_conv_silu(b_ref, pb_ref, cwb_ref, cbb_ref, bc_ref)
    _causal_conv_silu(c_ref, pc_ref, cwc_ref, cbc_ref, cc_ref)

    L = M_CHUNK
    row = lax.broadcasted_iota(jnp.int32, (L, L), 0)
    col = lax.broadcasted_iota(jnp.int32, (L, L), 1)
    causal = col <= row
    ltri = causal.astype(F32)

    dt = _softplus(dtr_ref[...] + dtb_ref[...])
    dta = dt * (-jnp.exp(alog_ref[...]))
    acs = jnp.dot(ltri, dta, preferred_element_type=F32, precision=HIGHEST)
    dt_t = dt.T
    acs_t = acs.T
    e = e_ref[...]
    dt_exp = jnp.dot(dt, e, preferred_element_type=F32, precision=HIGHEST)
    acs_exp = jnp.dot(acs, e, preferred_element_type=F32, precision=HIGHEST)

    gw = M_HEAD_DIM * 8
    first_half = col < M_HEAD_DIM
    for g in range(M_GROUPS):
        bg = bc_ref[:, g * M_D_STATE:(g + 1) * M_D_STATE].astype(BF16)
        cg = cc_ref[:, g * M_D_STATE:(g + 1) * M_D_STATE].astype(BF16)
        xg = xc_ref[:, g * gw:(g + 1) * gw]
        ae = acs_exp[:, g * gw:(g + 1) * gw]
        de = dt_exp[:, g * gw:(g + 1) * gw]
        cb = lax.dot_general(cg, bg, (((1,), (1,)), ((), ())), preferred_element_type=F32)
        sg = state_ref[g]
        y_off = jnp.dot(cg, sg.astype(BF16), preferred_element_type=F32) * jnp.exp(ae)
        a_last = ae[L - 1:L, :]
        to_end = jnp.exp(a_last - ae) * de
        xw = (xg * to_end).astype(BF16)
        upd = lax.dot_general(bg, xw, (((0,), (0,)), ((), ())), preferred_element_type=F32)
        state_ref[g] = sg * jnp.exp(a_last) + upd
        for jp in range(4):
            xpair = xg[:, jp * LANES:(jp + 1) * LANES]
            ys = jnp.zeros((L, LANES), F32)
            for half in range(2):
                h = g * 8 + jp * 2 + half
                a_col = jnp.sum(jnp.where(col == h, acs, 0.0), axis=1, keepdims=True)
                a_row = acs_t[h:h + 1, :]
                d_row = dt_t[h:h + 1, :]
                decay = jnp.exp(jnp.where(causal, a_col - a_row, -jnp.inf))
                w = (cb * decay * d_row).astype(BF16)
                keep = first_half if half == 0 else jnp.logical_not(first_half)
                xm = jnp.where(keep, xpair, 0.0).astype(BF16)
                ys = ys + jnp.dot(w, xm, preferred_element_type=F32)
            lo = g * gw + jp * LANES
            y_ref[:, lo:lo + LANES] = (ys + y_off[:, jp * LANES:(jp + 1) * LANES]
                                       + xpair * dexp_ref[:, lo:lo + LANES])

    for g in range(M_GROUPS):
        sl = slice(g * gw, (g + 1) * gw)
        yf = y_ref[:, sl] * _silu(z_ref[:, sl])
        ms = jnp.mean(yf * yf, axis=1, keepdims=True)
        o_ref[:, sl] = (yf * lax.rsqrt(ms + M_NORM_EPS) * nw_ref[:, sl]).astype(o_ref.dtype)


def mamba_ssd(zx, dtr, conv_w, conv_b, dt_bias, a_log, d_skip, norm_w, batch, seq):
    t = zx.shape[0]
    d_inner = norm_w.shape[0]
    n_heads = d_inner // M_HEAD_DIM
    bc_w = M_GROUPS * M_D_STATE
    nc = seq // M_CHUNK
    assert d_inner % 2048 == 0 and bc_w == 1024 and zx.shape[1] == 2 * d_inner + 2 * bc_w

    pad_h = LANES - n_heads
    dtb = jnp.pad(dt_bias, (0, pad_h)).reshape(1, LANES)
    alog = jnp.pad(a_log, (0, pad_h)).reshape(1, LANES)
    dexp = jnp.repeat(d_skip, M_HEAD_DIM).reshape(1, d_inner)
    expand = (jnp.arange(LANES)[:, None] == (jnp.arange(d_inner)[None, :] // M_HEAD_DIM)).astype(F32)
    cwx, cwb, cwc = conv_w[:, :d_inner], conv_w[:, d_inner:d_inner + bc_w], conv_w[:, d_inner + bc_w:]
    cb2 = conv_b.reshape(1, -1)
    cbx, cbb, cbc = cb2[:, :d_inner], cb2[:, d_inner:d_inner + bc_w], cb2[:, d_inner + bc_w:]

    row_map = lambda b, c: (b * nc + c, 0)
    const = lambda b, c: (0, 0)
    xblk = d_inner // bc_w
    in_specs = [
        pl.BlockSpec((M_CHUNK, d_inner), row_map),
        pl.BlockSpec((M_CHUNK, d_inner), lambda b, c: (b * nc + c, 1)),
        pl.BlockSpec((M_CHUNK, bc_w), lambda b, c: (b * nc + c, 2 * xblk)),
        pl.BlockSpec((M_CHUNK, bc_w), lambda b, c: (b * nc + c, 2 * xblk + 1)),
        pl.BlockSpec((M_CHUNK, LANES), row_map),
        pl.BlockSpec((M_D_CONV, d_inner), const), pl.BlockSpec((M_D_CONV, bc_w), const),
        pl.BlockSpec((M_D_CONV, bc_w), const),
        pl.BlockSpec((1, d_inner), const), pl.BlockSpec((1, bc_w), const), pl.BlockSpec((1, bc_w), const),
        pl.BlockSpec((1, LANES), const), pl.BlockSpec((1, LANES), const),
        pl.BlockSpec((1, d_inner), const), pl.BlockSpec((1, d_inner), const),
        pl.BlockSpec((LANES, d_inner), const),
    ]
    return pl.pallas_call(
        _mamba_body,
        grid=(batch, nc),
        in_specs=in_specs,
        out_specs=pl.BlockSpec((M_CHUNK, d_inner), row_map),
        out_shape=jax.ShapeDtypeStruct((t, d_inner), BF16),
        scratch_shapes=[
            pltpu.VMEM((M_GROUPS, M_D_STATE, M_HEAD_DIM * 8), F32),
            pltpu.VMEM((SUBLANES, d_inner), F32), pltpu.VMEM((SUBLANES, bc_w), F32),
            pltpu.VMEM((SUBLANES, bc_w), F32),
            pltpu.VMEM((M_CHUNK, d_inner), F32), pltpu.VMEM((M_CHUNK, bc_w), F32),
            pltpu.VMEM((M_CHUNK, bc_w), F32),
            pltpu.VMEM((M_CHUNK, d_inner), F32),
        ],
        compiler_params=_params("arbitrary", "arbitrary"),
        name="mamba_ssd",
    )(zx, zx, zx, zx, dtr, cwx, cwb, cwc, cbx, cbb, cbc, dtb, alog, dexp, norm_w.reshape(1, d_inner), expand)


def _attn_body(q_ref, k_ref, v_ref, o_ref, acc_ref, *, scale):
    qt = pl.program_id(2)
    tq, blk = SB_Q_TILE, SB_BLOCK
    per_tile = tq // blk
    q = (q_ref[...] * scale).astype(BF16)
    row = lax.broadcasted_iota(jnp.int32, (tq, blk), 0)
    col = lax.broadcasted_iota(jnp.int32, (tq, blk), 1)
    r2 = lax.broadcasted_iota(jnp.int32, (blk, blk), 0)
    c2 = lax.broadcasted_iota(jnp.int32, (blk, blk), 1)
    later = (r2 > c2).astype(BF16)
    acc_ref[...] = jnp.zeros_like(acc_ref)

    def make_step(first_kb, masked):
        def step(i, tail):
            kb = first_kb - i
            start = pl.multiple_of(kb * blk, blk)
            k = k_ref[pl.ds(start, blk), :]
            v = v_ref[pl.ds(start, blk), :]
            z = lax.dot_general(q, k, (((1,), (1,)), ((), ())), preferred_element_type=F32)
            lp = jnp.log1p(jnp.exp(-jnp.abs(z)))
            log_1mb = -(jnp.maximum(z, 0.0) + lp)
            log_b = z + log_1mb
            if masked:
                strict = (kb * blk + col) < (qt * tq + row)
                log_1mb = jnp.where(strict, log_1mb, 0.0)
            hi = log_1mb.astype(BF16)
            lo = (log_1mb - hi.astype(F32)).astype(BF16)
            within = (jnp.dot(hi, later, preferred_element_type=F32)
                      + jnp.dot(lo, later, preferred_element_type=F32))
            a = jnp.exp(log_b + within + tail)
            if masked:
                a = jnp.where(strict, a, 0.0)
            acc_ref[...] += jnp.dot(a.astype(BF16), v, preferred_element_type=F32)
            return tail + jnp.sum(log_1mb, axis=1, keepdims=True)
        return step

    tail = lax.fori_loop(0, per_tile, make_step(qt * per_tile + per_tile - 1, True), jnp.zeros((tq, 1), F32))
    lax.fori_loop(0, qt * per_tile, make_step(qt * per_tile - 1, False), tail)
    o_ref[...] = acc_ref[...].astype(o_ref.dtype)


def stick_breaking_attention(q, kv, batch, seq):
    t, width = q.shape
    n_heads = width // SB_HEAD_DIM
    nq = seq // SB_Q_TILE
    return pl.pallas_call(
        functools.partial(_attn_body, scale=1.0 / math.sqrt(SB_HEAD_DIM)),
        grid=(batch, n_heads, nq),
        in_specs=[
            pl.BlockSpec((SB_Q_TILE, SB_HEAD_DIM), lambda b, h, i: (b * nq + i, h)),
            pl.BlockSpec((seq, SB_HEAD_DIM), lambda b, h, i: (b, h)),
            pl.BlockSpec((seq, SB_HEAD_DIM), lambda b, h, i: (b, n_heads + h)),
        ],
        out_specs=pl.BlockSpec((SB_Q_TILE, SB_HEAD_DIM), lambda b, h, i: (b * nq + i, h)),
        out_shape=jax.ShapeDtypeStruct((t, width), BF16),
        scratch_shapes=[pltpu.VMEM((SB_Q_TILE, SB_HEAD_DIM), F32)],
        compiler_params=_params("arbitrary", "arbitrary", "arbitrary"),
        name="stick_breaking_attention",
    )(q, kv, kv)


def _top16_rows(s, n_rows):
    tokens = s.shape[1]
    row = lax.broadcasted_iota(jnp.int32, s.shape, 0)
    slot = lax.broadcasted_iota(jnp.int32, (PEER_TOPK, tokens), 0)
    vals = jnp.zeros((PEER_TOPK, tokens), F32)
    idxs = jnp.zeros((PEER_TOPK, tokens), jnp.int32)
    for r in range(PEER_TOPK):
        m = jnp.max(s, axis=0, keepdims=True)
        first = jnp.min(jnp.where(s == m, row, n_rows), axis=0, keepdims=True)
        s = jnp.where(row == first, -jnp.inf, s)
        vals = jnp.where(slot == r, m, vals)
        idxs = jnp.where(slot == r, first, idxs)
    return vals, idxs


def _peer_topk_body(q_ref, k1_ref, k2_ref, idx_ref, gate_ref):
    k1 = k1_ref[...]
    k2 = k2_ref[...]
    nt = (((1,), (1,)), ((), ()))
    for h in range(PEER_HEADS):
        base = h * 2 * PEER_HALF
        q1 = q_ref[:, base:base + PEER_HALF]
        q2 = q_ref[:, base + PEER_HALF:base + 2 * PEER_HALF]
        s1 = lax.dot_general(k1, q1, nt, preferred_element_type=F32, precision=HIGHEST)
        s2 = lax.dot_general(k2, q2, nt, preferred_element_type=F32, precision=HIGHEST)
        v1, i1 = _top16_rows(s1, PEER_N_KEYS)
        v2, i2 = _top16_rows(s2, PEER_N_KEYS)
        cand = jnp.concatenate([v1[a:a + 1, :] + v2 for a in range(PEER_TOPK)], axis=0)
        sc, ci = _top16_rows(cand, PEER_TOPK * PEER_TOPK)
        ca = ci // PEER_TOPK
        cbi = ci % PEER_TOPK
        e1 = jnp.zeros_like(ci)
        e2 = jnp.zeros_like(ci)
        for a in range(PEER_TOPK):
            e1 = jnp.where(ca == a, i1[a:a + 1, :], e1)
            e2 = jnp.where(cbi == a, i2[a:a + 1, :], e2)
        p = jnp.exp(sc - jnp.max(sc, axis=0, keepdims=True))
        gate = p / jnp.sum(p, axis=0, keepdims=True)
        idx_ref[h * PEER_TOPK:(h + 1) * PEER_TOPK, :] = e1 * PEER_N_KEYS + e2
        gate_ref[h * PEER_TOPK:(h + 1) * PEER_TOPK, :] = gate


def peer_topk(q, sub_keys, tt=256):
    t, width = q.shape
    assert width == PEER_HEADS * 2 * PEER_HALF
    return pl.pallas_call(
        _peer_topk_body,
        grid=(t // tt,),
        in_specs=[
            pl.BlockSpec((tt, width), lambda i: (i, 0)),
            pl.BlockSpec((PEER_N_KEYS, PEER_HALF), lambda i: (0, 0)),
            pl.BlockSpec((PEER_N_KEYS, PEER_HALF), lambda i: (0, 0)),
        ],
        out_specs=[pl.BlockSpec((PEER_SLOTS, tt), lambda i: (0, i)),
                   pl.BlockSpec((PEER_SLOTS, tt), lambda i: (0, i))],
        out_shape=[jax.ShapeDtypeStruct((PEER_SLOTS, t), jnp.int32),
                   jax.ShapeDtypeStruct((PEER_SLOTS, t), F32)],
        compiler_params=_params("arbitrary"),
        name="peer_topk",
    )(q, sub_keys[0], sub_keys[1])


def _pack_pairs(x):
    half = x.shape[1] // 2
    bits = lax.bitcast_convert_type(x.astype(BF16).astype(F32), U32)
    return (bits[:, :half] >> 16) | (bits[:, half:] & jnp.uint32(0xFFFF0000))


def _unpack_pairs(w):
    lo = lax.bitcast_convert_type(w << 16, F32)
    hi = lax.bitcast_convert_type(w & jnp.uint32(0xFFFF0000), F32)
    return lo, hi


def _pack_tables_body(u_ref, v_ref, o_ref):
    half = u_ref.shape[1] // 2
    o_ref[:, :half] = _pack_pairs(u_ref[...])
    o_ref[:, half:] = _pack_pairs(v_ref[...])


def pack_tables(u_tab, v_tab, tm=512):
    n, d = u_tab.shape
    return pl.pallas_call(
        _pack_tables_body,
        grid=(n // tm,),
        in_specs=[pl.BlockSpec((tm, d), lambda i: (i, 0)), pl.BlockSpec((tm, d), lambda i: (i, 0))],
        out_specs=pl.BlockSpec((tm, d), lambda i: (i, 0)),
        out_shape=jax.ShapeDtypeStruct((n, d), U32),
        compiler_params=_params("arbitrary"),
        name="peer_pack_tables",
    )(u_tab, v_tab)


def _gelu(x):
    return 0.5 * x * (1.0 + lax.erf(x * (1.0 / math.sqrt(2.0))))


def _peer_mix_body(idx_hbm, x_ref, gate_ref, h_ref, tab_hbm, o_ref,
                   idx_smem, buf, mix_ref, idx_sem, row_sem, *, tb):
    step = pl.program_id(0)
    n_steps = pl.num_programs(0)
    n_win = (tb + PEER_BUFS) * PEER_SLOTS
    off = pl.multiple_of(step * (tb * PEER_SLOTS), tb * PEER_SLOTS)
    idx_copy = pltpu.make_async_copy(idx_hbm.at[pl.ds(off, n_win)], idx_smem, idx_sem)
    idx_copy.start()
    idx_copy.wait()

    def row_copy(j, slot, k):
        e = idx_smem[j * PEER_SLOTS + k]
        return pltpu.make_async_copy(tab_hbm.at[e], buf.at[slot, :, pl.ds(k, 1), :], row_sem.at[slot])

    def wait(slot):
        pltpu.make_async_copy(buf.at[slot], buf.at[slot], row_sem.at[slot]).wait()

    @pl.when(step == 0)
    def _():
        for j in range(PEER_AHEAD):
            def body(k, carry):
                row_copy(j, j, k).start()
                return carry
            lax.fori_loop(0, PEER_SLOTS, body, 0, unroll=8)

    d = x_ref.shape[1]
    half_chunks = d // (2 * LANES)
    row = lax.broadcasted_iota(jnp.int32, (PEER_SLOTS, LANES), 0)
    col = lax.broadcasted_iota(jnp.int32, (PEER_SLOTS, LANES), 1)
    eye = row == col
    per_chunk = PEER_SLOTS // (2 * half_chunks)

    def group(g, carry):
        base = pl.multiple_of(g * SUBLANES, SUBLANES)
        rows = pl.ds(base, SUBLANES)
        gate8 = gate_ref[rows, :]
        for r in range(SUBLANES):
            ahead_slot = (r + PEER_AHEAD) % PEER_BUFS
            started = [0]

            def start_some():
                for k in range(started[0], started[0] + per_chunk):
                    row_copy(base + r + PEER_AHEAD, ahead_slot, k).start()
                started[0] += per_chunk

            wait(r)
            acc = jnp.zeros((PEER_SLOTS, LANES), F32)
            for c in range(half_chunks):
                lo, hi = _unpack_pairs(buf[r, c])
                x_lo = x_ref[rows, c * LANES:(c + 1) * LANES][r:r + 1, :]
                x_hi = x_ref[rows, (half_chunks + c) * LANES:(half_chunks + c + 1) * LANES][r:r + 1, :]
                acc = acc + lo * x_lo + hi * x_hi
                start_some()
            hid = jnp.sum(acc, axis=1, keepdims=True)
            gate_row = jnp.broadcast_to(gate8[r:r + 1, :], (PEER_SLOTS, LANES))
            gate = jnp.sum(jnp.where(eye, gate_row, 0.0), axis=1, keepdims=True)
            w = jnp.broadcast_to(gate * _gelu(hid), (PEER_SLOTS, LANES))
            for c in range(half_chunks):
                lo, hi = _unpack_pairs(buf[r, half_chunks + c])
                mix_ref[r:r + 1, c * LANES:(c + 1) * LANES] = jnp.sum(lo * w, axis=0, keepdims=True)
                mix_ref[r:r + 1, (half_chunks + c) * LANES:(half_chunks + c + 1) * LANES] = (
                    jnp.sum(hi * w, axis=0, keepdims=True))
                start_some()
        o_ref[rows, :] = h_ref[rows, :] + mix_ref[...]
        return carry

    lax.fori_loop(0, tb // SUBLANES, group, 0)

    @pl.when(step == n_steps - 1)
    def _():
        for j in range(PEER_AHEAD):
            wait(j)


def peer_mix(xn, idx_t, gate_t, h, table, tb=256):
    t, d = xn.shape
    assert idx_t.shape == (PEER_SLOTS, t) and tb % PEER_BUFS == 0 and PEER_BUFS == SUBLANES
    assert PEER_AHEAD < PEER_BUFS and PEER_SLOTS % (d // LANES) == 0
    idx = jnp.pad(idx_t.T.reshape(-1), (0, PEER_BUFS * PEER_SLOTS))
    n_chunks = d // LANES
    table = table.reshape(table.shape[0], n_chunks, 1, LANES)
    blk = lambda i: (i, 0)
    return pl.pallas_call(
        functools.partial(_peer_mix_body, tb=tb),
        grid=(t // tb,),
        in_specs=[
            pl.BlockSpec(memory_space=pl.ANY),
            pl.BlockSpec((tb, d), blk),
            pl.BlockSpec((tb, PEER_SLOTS), blk),
            pl.BlockSpec((tb, d), blk),
            pl.BlockSpec(memory_space=pl.ANY),
        ],
        out_specs=pl.BlockSpec((tb, d), blk),
        out_shape=jax.ShapeDtypeStruct((t, d), F32),
        scratch_shapes=[
            pltpu.SMEM(((tb + PEER_BUFS) * PEER_SLOTS,), jnp.int32),
            pltpu.VMEM((PEER_BUFS, n_chunks, PEER_SLOTS, LANES), U32),
            pltpu.VMEM((SUBLANES, d), F32),
            pltpu.SemaphoreType.DMA(()),
            pltpu.SemaphoreType.DMA((PEER_BUFS,)),
        ],
        compiler_params=_params("arbitrary"),
        name="peer_mix",
    )(idx, xn, gate_t.T, h, table)


def peer_layer(h, norm_g, w_q, sub_keys, u_tab, v_tab):
    xn, = rmsnorm(h, norm_g, [F32])
    q = matmul(xn, w_q, precision=HIGHEST, tm=512, name="peer_query")
    idx_t, gate_t = peer_topk(q, sub_keys)
    return peer_mix(xn, idx_t, gate_t, h, pack_tables(u_tab, v_tab))


def kernel(x, norm_mix, norm_ffn, norm_final, m_in_proj, m_conv_w, m_conv_b, m_dt_bias, m_a_log, m_d, m_norm,
           m_out_proj, kv_norm, w_kv, sb_w_q, sb_w_o, peer_w_q, peer_sub_keys, peer_u, peer_v):
    batch, seq, d_model = x.shape
    t = batch * seq
    h = x.reshape(t, d_model)

    n_heads = m_dt_bias.shape[1]
    proj_w = m_in_proj.shape[2] - n_heads
    u, = rmsnorm(h, norm_mix[0], [BF16])
    w_in = m_in_proj[0]
    zx = matmul(u, w_in[:, :proj_w].astype(BF16), tn=1024, name="in_proj")
    w_dt = jnp.pad(w_in[:, proj_w:], ((0, 0), (0, LANES - n_heads))).astype(BF16)
    dtr = matmul(u, w_dt, name="in_proj_dt")
    y = mamba_ssd(zx, dtr, m_conv_w[0], m_conv_b[0], m_dt_bias[0], m_a_log[0], m_d[0], m_norm[0], batch, seq)
    h = matmul(y, m_out_proj[0].astype(BF16), residual=h, name="out_proj")
    h = peer_layer(h, norm_ffn[0], peer_w_q[0], peer_sub_keys[0], peer_u[0], peer_v[0])

    kvn, = rmsnorm(h, kv_norm, [BF16])
    qn, = rmsnorm(h, norm_mix[1], [BF16])
    kv = matmul(kvn, w_kv.astype(BF16), out_dtype=BF16, name="kv_proj")
    q = matmul(qn, sb_w_q[0].astype(BF16), name="q_proj")
    o = stick_breaking_attention(q, kv, batch, seq)
    h = matmul(o, sb_w_o[0].astype(BF16), residual=h, name="o_proj")
    h = peer_layer(h, norm_ffn[1], peer_w_q[1], peer_sub_keys[1], peer_u[1], peer_v[1])

    out, = rmsnorm(h, norm_final, [F32])
    return out.reshape(batch, seq, d_model)
```

```python
import functools
import math

import jax
import jax.numpy as jnp
from jax import lax
from jax.experimental import pallas as pl
from jax.experimental.pallas import tpu as pltpu

F32 = jnp.float32
BF16 = jnp.bfloat16
U32 = jnp.uint32
HIGHEST = lax.Precision.HIGHEST

NORM_EPS = 1e-6
LANES = 128
SUBLANES = 8

M_HEAD_DIM = 64
M_D_STATE = 128
M_GROUPS = 8
M_D_CONV = 4
M_CHUNK = 128
M_NORM_EPS = 1e-5

SB_HEAD_DIM = 128
SB_BLOCK = 128
SB_Q_TILE = 256
SB_HEADS_PER_STEP = 8

PEER_HEADS = 8
PEER_N_KEYS = 128
PEER_TOPK = 16
PEER_HALF = 128
PEER_SLOTS = PEER_HEADS * PEER_TOPK
PEER_BUFS = 8
PEER_AHEAD = 6

VMEM_LIMIT = 48 * 1024 * 1024


def _params(*sem):
    return pltpu.CompilerParams(dimension_semantics=sem, vmem_limit_bytes=VMEM_LIMIT)


def _rmsnorm_body(x_ref, g_ref, *o_refs, eps, gain_of_out):
    x = x_ref[...]
    xn = x * lax.rsqrt(jnp.mean(x * x, axis=-1, keepdims=True) + eps)
    for o_ref, gi in zip(o_refs, gain_of_out):
        o_ref[...] = (xn * g_ref[gi:gi + 1, :]).astype(o_ref.dtype)


def _split_bf16(x):
    hi = x.astype(BF16)
    return hi, (x - hi.astype(F32)).astype(BF16)


def rmsnorm(x, gains, outs, tm=256):
    t, d = x.shape
    g = jnp.stack(gains)
    blk = pl.BlockSpec((tm, d), lambda i: (i, 0))
    return pl.pallas_call(
        functools.partial(_rmsnorm_body, eps=NORM_EPS, gain_of_out=tuple(gi for gi, _ in outs)),
        grid=(t // tm,),
        in_specs=[blk, pl.BlockSpec(g.shape, lambda i: (0, 0))],
        out_specs=[blk for _ in outs],
        out_shape=[jax.ShapeDtypeStruct((t, d), dt) for _, dt in outs],
        compiler_params=_params("arbitrary"),
        name="rmsnorm",
    )(x, g)


def _mm_body(a_ref, w_ref, *rest):
    o_ref, wb_ref = rest[-2:]

    @pl.when(pl.program_id(1) == 0)
    def _():
        wb_ref[...] = w_ref[...].astype(BF16)

    acc = jnp.dot(a_ref[...], wb_ref[...], preferred_element_type=F32)
    if len(rest) == 3:
        acc = rest[0][...] + acc
    o_ref[...] = acc.astype(o_ref.dtype)


def matmul(a, w, n=None, residual=None, out_dtype=F32, tm=1024, tn=1024, name="matmul"):
    m, k = a.shape
    n = w.shape[1] if n is None else n
    tn = min(tn, n)
    assert m % tm == 0 and n % tn == 0
    in_specs = [pl.BlockSpec((tm, k), lambda j, i: (i, 0)), pl.BlockSpec((k, tn), lambda j, i: (0, j))]
    args = [a, w]
    if residual is not None:
        in_specs.append(pl.BlockSpec((tm, tn), lambda j, i: (i, j)))
        args.append(residual)
    return pl.pallas_call(
        _mm_body,
        grid=(n // tn, m // tm),
        in_specs=in_specs,
        out_specs=pl.BlockSpec((tm, tn), lambda j, i: (i, j)),
        out_shape=jax.ShapeDtypeStruct((m, n), out_dtype),
        scratch_shapes=[pltpu.VMEM((k, tn), BF16)],
        compiler_params=_params("arbitrary", "arbitrary"),
        name=name,
    )(*args)


def _softplus(x):
    return jnp.maximum(x, 0.0) + jnp.log1p(jnp.exp(-jnp.abs(x)))


def _silu(x):
    return x * jax.nn.sigmoid(x)


def _causal_conv_silu(raw_ref, prev_ref, w_ref, bias_ref, out_ref):
    cur = raw_ref[...]
    prev8 = prev_ref[...]
    w = w_ref[...]
    bias = bias_ref[...]
    top = cur[0:SUBLANES, :]
    row8 = lax.broadcasted_iota(jnp.int32, top.shape, 0)
    acc = cur * w[M_D_CONV - 1:M_D_CONV, :]
    acc_top = top * w[M_D_CONV - 1:M_D_CONV, :]
    for d in range(1, M_D_CONV):
        wd = w[M_D_CONV - 1 - d:M_D_CONV - d, :]
        shifted = pltpu.roll(cur, d, 0)
        acc = acc + shifted * wd
        shifted_top = jnp.where(row8 < d, pltpu.roll(prev8, d, 0), shifted[0:SUBLANES, :])
        acc_top = acc_top + shifted_top * wd
    out_ref[...] = _silu(acc + bias)
    out_ref[0:SUBLANES, :] = _silu(acc_top + bias)
    prev_ref[...] = cur[M_CHUNK - SUBLANES:M_CHUNK, :]


def _mamba_body(z_ref, x_ref, b_ref, c_ref, dtr_ref, cwx_ref, cwb_ref, cwc_ref, cbx_ref, cbb_ref, cbc_ref,
                dtb_ref, alog_ref, dexp_ref, nw_ref, e_ref, o_ref,
                state_ref, px_ref, pb_ref, pc_ref, xc_ref, bc_ref, cc_ref, y_ref):
    @pl.when(pl.program_id(1) == 0)
    def _():
        state_ref[...] = jnp.zeros_like(state_ref)
        px_ref[...] = jnp.zeros_like(px_ref)
        pb_ref[...] = jnp.zeros_like(pb_ref)
        pc_ref[...] = jnp.zeros_like(pc_ref)

    _causal_conv_silu(x_ref, px_ref, cwx_ref, cbx_ref, xc_ref)
    _causal_conv_silu(b_ref, pb_ref, cwb_ref, cbb_ref, bc_ref)
    _causal_conv_silu(c_ref, pc_ref, cwc_ref, cbc_ref, cc_ref)

    L = M_CHUNK
    row = lax.broadcasted_iota(jnp.int32, (L, L), 0)
    col = lax.broadcasted_iota(jnp.int32, (L, L), 1)
    causal = col <= row
    ltri = causal.astype(F32)

    dt = _softplus(dtr_ref[...] + dtb_ref[...])
    dta = dt * (-jnp.exp(alog_ref[...]))
    acs = jnp.dot(ltri, dta, preferred_element_type=F32, precision=HIGHEST)
    dt_t = dt.T
    acs_t = acs.T
    e = e_ref[...]

    def expand(v):
        hi, rest = v.astype(BF16), v
        rest = rest - hi.astype(F32)
        mid = rest.astype(BF16)
        lo = (rest - mid.astype(F32)).astype(BF16)
        return (jnp.dot(hi, e, preferred_element_type=F32) + jnp.dot(mid, e, preferred_element_type=F32)
                + jnp.dot(lo, e, preferred_element_type=F32))

    dt_exp = expand(dt)
    acs_exp = expand(acs)

    gw = M_HEAD_DIM * 8
    first_half = col < M_HEAD_DIM
    for g in range(M_GROUPS):
        bg = bc_ref[:, g * M_D_STATE:(g + 1) * M_D_STATE].astype(BF16)
        cg = cc_ref[:, g * M_D_STATE:(g + 1) * M_D_STATE].astype(BF16)
        xg = xc_ref[:, g * gw:(g + 1) * gw]
        ae = acs_exp[:, g * gw:(g + 1) * gw]
        de = dt_exp[:, g * gw:(g + 1) * gw]
        cb = lax.dot_general(cg, bg, (((1,), (1,)), ((), ())), preferred_element_type=F32)
        sg = state_ref[g]
        y_off = jnp.dot(cg, sg.astype(BF16), preferred_element_type=F32) * jnp.exp(ae)
        a_last = ae[L - 1:L, :]
        to_end = jnp.exp(a_last - ae) * de
        xw = (xg * to_end).astype(BF16)
        upd = lax.dot_general(bg, xw, (((0,), (0,)), ((), ())), preferred_element_type=F32)
        state_ref[g] = sg * jnp.exp(a_last) + upd
        for jp in range(4):
            xpair = xg[:, jp * LANES:(jp + 1) * LANES]
            ys = jnp.zeros((L, LANES), F32)
            for half in range(2):
                h = g * 8 + jp * 2 + half
                a_col = jnp.sum(jnp.where(col == h, acs, 0.0), axis=1, keepdims=True)
                a_row = acs_t[h:h + 1, :]
                d_row = dt_t[h:h + 1, :]
                decay = jnp.exp(jnp.where(causal, a_col - a_row, -jnp.inf))
                w = (cb * decay * d_row).astype(BF16)
                keep = first_half if half == 0 else jnp.logical_not(first_half)
                xm = jnp.where(keep, xpair, 0.0).astype(BF16)
                ys = ys + jnp.dot(w, xm, preferred_element_type=F32)
            lo = g * gw + jp * LANES
            y_ref[:, lo:lo + LANES] = (ys + y_off[:, jp * LANES:(jp + 1) * LANES]
                                       + xpair * dexp_ref[:, lo:lo + LANES])

    for g in range(M_GROUPS):
        sl = slice(g * gw, (g + 1) * gw)
        yf = y_ref[:, sl] * _silu(z_ref[:, sl])
        ms = jnp.mean(yf * yf, axis=1, keepdims=True)
        o_ref[:, sl] = (yf * lax.rsqrt(ms + M_NORM_EPS) * nw_ref[:, sl]).astype(o_ref.dtype)


def mamba_ssd(zx, dtr, conv_w, conv_b, dt_bias, a_log, d_skip, norm_w, batch, seq):
    t = zx.shape[0]
    d_inner = norm_w.shape[0]
    n_heads = d_inner // M_HEAD_DIM
    bc_w = M_GROUPS * M_D_STATE
    nc = seq // M_CHUNK
    assert d_inner % 2048 == 0 and bc_w == 1024 and zx.shape[1] == 2 * d_inner + 2 * bc_w

    pad_h = LANES - n_heads
    dtb = jnp.pad(dt_bias, (0, pad_h)).reshape(1, LANES)
    alog = jnp.pad(a_log, (0, pad_h)).reshape(1, LANES)
    dexp = jnp.repeat(d_skip, M_HEAD_DIM).reshape(1, d_inner)
    expand = (jnp.arange(LANES)[:, None] == (jnp.arange(d_inner)[None, :] // M_HEAD_DIM)).astype(BF16)
    cwx, cwb, cwc = conv_w[:, :d_inner], conv_w[:, d_inner:d_inner + bc_w], conv_w[:, d_inner + bc_w:]
    cb2 = conv_b.reshape(1, -1)
    cbx, cbb, cbc = cb2[:, :d_inner], cb2[:, d_inner:d_inner + bc_w], cb2[:, d_inner + bc_w:]

    row_map = lambda b, c: (b * nc + c, 0)
    const = lambda b, c: (0, 0)
    xblk = d_inner // bc_w
    in_specs = [
        pl.BlockSpec((M_CHUNK, d_inner), row_map),
        pl.BlockSpec((M_CHUNK, d_inner), lambda b, c: (b * nc + c, 1)),
        pl.BlockSpec((M_CHUNK, bc_w), lambda b, c: (b * nc + c, 2 * xblk)),
        pl.BlockSpec((M_CHUNK, bc_w), lambda b, c: (b * nc + c, 2 * xblk + 1)),
        pl.BlockSpec((M_CHUNK, LANES), row_map),
        pl.BlockSpec((M_D_CONV, d_inner), const), pl.BlockSpec((M_D_CONV, bc_w), const),
        pl.BlockSpec((M_D_CONV, bc_w), const),
        pl.BlockSpec((1, d_inner), const), pl.BlockSpec((1, bc_w), const), pl.BlockSpec((1, bc_w), const),
        pl.BlockSpec((1, LANES), const), pl.BlockSpec((1, LANES), const),
        pl.BlockSpec((1, d_inner), const), pl.BlockSpec((1, d_inner), const),
        pl.BlockSpec((LANES, d_inner), const),
    ]
    return pl.pallas_call(
        _mamba_body,
        grid=(batch, nc),
        in_specs=in_specs,
        out_specs=pl.BlockSpec((M_CHUNK, d_inner), row_map),
        out_shape=jax.ShapeDtypeStruct((t, d_inner), BF16),
        scratch_shapes=[
            pltpu.VMEM((M_GROUPS, M_D_STATE, M_HEAD_DIM * 8), F32),
            pltpu.VMEM((SUBLANES, d_inner), F32), pltpu.VMEM((SUBLANES, bc_w), F32),
            pltpu.VMEM((SUBLANES, bc_w), F32),
            pltpu.VMEM((M_CHUNK, d_inner), F32), pltpu.VMEM((M_CHUNK, bc_w), F32),
            pltpu.VMEM((M_CHUNK, bc_w), F32),
            pltpu.VMEM((M_CHUNK, d_inner), F32),
        ],
        compiler_params=_params("arbitrary", "arbitrary"),
        name="mamba_ssd",
    )(zx, zx, zx, zx, dtr, cwx, cwb, cwc, cbx, cbb, cbc, dtb, alog, dexp, norm_w.reshape(1, d_inner), expand)


def _attn_body(q_ref, k_ref, v_ref, o_ref, *scratch, scale):
    acc_refs = scratch[:SB_HEADS_PER_STEP]
    tail_refs = scratch[SB_HEADS_PER_STEP:]
    qt = pl.program_id(2)
    tq, blk, hd = SB_Q_TILE, SB_BLOCK, SB_HEAD_DIM
    per_tile = tq // blk
    row = lax.broadcasted_iota(jnp.int32, (tq, blk), 0)
    col = lax.broadcasted_iota(jnp.int32, (tq, blk), 1)
    r2 = lax.broadcasted_iota(jnp.int32, (blk, 2 * blk), 0)
    c2 = lax.broadcasted_iota(jnp.int32, (blk, 2 * blk), 1)
    sums = ((r2 > c2) | (c2 >= blk)).astype(BF16)
    for ref in scratch:
        ref[...] = jnp.zeros_like(ref)

    def make_step(first_kb, masked):
        def step(i, carry):
            kb = first_kb - i
            start = pl.multiple_of(kb * blk, blk)
            if masked:
                strict = (kb * blk + col) < (qt * tq + row)
            heads = range(SB_HEADS_PER_STEP)
            lanes = [slice(h * hd, (h + 1) * hd) for h in heads]
            z = [lax.dot_general((q_ref[:, lanes[h]] * scale).astype(BF16), k_ref[pl.ds(start, blk), lanes[h]],
                                 (((1,), (1,)), ((), ())), preferred_element_type=F32) for h in heads]
            log_1mb, log_b, s = [], [], []
            for h in heads:
                lp = jnp.log(1.0 + jnp.exp(-jnp.abs(z[h])))
                l1 = -(jnp.maximum(z[h], 0.0) + lp)
                log_b.append(z[h] + l1)
                log_1mb.append(jnp.where(strict, l1, 0.0) if masked else l1)
            for h in heads:
                hi, lo = _split_bf16(log_1mb[h])
                s.append(jnp.dot(hi, sums, preferred_element_type=F32)
                         + jnp.dot(lo, sums, preferred_element_type=F32))
            for h in heads:
                tail = tail_refs[h][...]
                a = jnp.exp(log_b[h] + s[h][:, :blk] + tail)
                if masked:
                    a = jnp.where(strict, a, 0.0)
                acc_refs[h][...] += jnp.dot(a.astype(BF16), v_ref[pl.ds(start, blk), lanes[h]],
                                            preferred_element_type=F32)
                tail_refs[h][...] = tail + s[h][:, blk:]
            return carry
        return step

    lax.fori_loop(0, per_tile, make_step(qt * per_tile + per_tile - 1, True), 0)
    lax.fori_loop(0, qt * per_tile, make_step(qt * per_tile - 1, False), 0)
    for h in range(SB_HEADS_PER_STEP):
        o_ref[:, h * hd:(h + 1) * hd] = acc_refs[h][...].astype(o_ref.dtype)


def stick_breaking_attention(q, kv, batch, seq):
    t, width = q.shape
    nq = seq // SB_Q_TILE
    gw = SB_HEADS_PER_STEP * SB_HEAD_DIM
    n_groups = width // gw
    return pl.pallas_call(
        functools.partial(_attn_body, scale=1.0 / math.sqrt(SB_HEAD_DIM)),
        grid=(batch, n_groups, nq),
        in_specs=[
            pl.BlockSpec((SB_Q_TILE, gw), lambda b, g, i: (b * nq + i, g)),
            pl.BlockSpec((seq, gw), lambda b, g, i: (b, g)),
            pl.BlockSpec((seq, gw), lambda b, g, i: (b, n_groups + g)),
        ],
        out_specs=pl.BlockSpec((SB_Q_TILE, gw), lambda b, g, i: (b * nq + i, g)),
        out_shape=jax.ShapeDtypeStruct((t, width), BF16),
        scratch_shapes=([pltpu.VMEM((SB_Q_TILE, SB_HEAD_DIM), F32)] * SB_HEADS_PER_STEP
                        + [pltpu.VMEM((SB_Q_TILE, SB_BLOCK), F32)] * SB_HEADS_PER_STEP),
        compiler_params=_params("arbitrary", "arbitrary", "arbitrary"),
        name="stick_breaking_attention",
    )(q, kv, kv)


def _top16_rows(s, ids):
    tokens = s.shape[1]
    slot = lax.broadcasted_iota(jnp.int32, (PEER_TOPK, tokens), 0)
    vals = jnp.zeros((PEER_TOPK, tokens), F32)
    wins = jnp.zeros((PEER_TOPK, tokens), F32)
    for r in range(PEER_TOPK):
        m = jnp.max(s, axis=0, keepdims=True)
        first = jnp.min(jnp.where(s == m, ids, jnp.inf), axis=0, keepdims=True)
        s = jnp.where(ids == first, -jnp.inf, s)
        vals = jnp.where(slot == r, m, vals)
        wins = jnp.where(slot == r, first, wins)
    return vals, wins


def _pair_candidates(v1, v2):
    k = PEER_TOPK
    tokens = v1.shape[1]
    sub = lax.broadcasted_iota(jnp.int32, (SUBLANES, tokens), 0)
    sub16 = lax.broadcasted_iota(jnp.int32, (k, tokens), 0)
    vals, ids = [], []
    vals.append(v1[0:1, :] + v2)
    ids.append(sub16)
    vals.append(jnp.where(sub16 >= 1, v1 + v2[0:1, :], -jnp.inf))
    ids.append(sub16 * k)
    for a in range(1, 5):
        b_max = k // (a + 1) - 1
        vals.append(jnp.where((sub >= 1) & (sub <= b_max), v1[a:a + 1, :] + v2[0:SUBLANES, :], -jnp.inf))
        ids.append(sub + a * k)
    vals.append(jnp.where((sub >= 5) & (sub <= 7), v1[0:SUBLANES, :] + v2[1:2, :], -jnp.inf))
    ids.append(sub * k + 1)
    return jnp.concatenate(vals, axis=0), jnp.concatenate(ids, axis=0).astype(F32)


def _peer_topk_body(q_ref, k1_ref, k2_ref, idx_ref, gate_ref):
    k1 = k1_ref[...]
    k2 = k2_ref[...]
    nt = (((1,), (1,)), ((), ()))
    key_ids = lax.broadcasted_iota(jnp.int32, (PEER_N_KEYS, q_ref.shape[0]), 0).astype(F32)
    for h in range(PEER_HEADS):
        base = h * 2 * PEER_HALF
        q1 = q_ref[:, base:base + PEER_HALF]
        q2 = q_ref[:, base + PEER_HALF:base + 2 * PEER_HALF]
        s1 = lax.dot_general(k1, q1, nt, preferred_element_type=F32, precision=HIGHEST)
        s2 = lax.dot_general(k2, q2, nt, preferred_element_type=F32, precision=HIGHEST)
        v1, i1 = _top16_rows(s1, key_ids)
        v2, i2 = _top16_rows(s2, key_ids)
        sc, ci = _top16_rows(*_pair_candidates(v1, v2))
        ci = ci.astype(jnp.int32)
        i1 = i1.astype(jnp.int32)
        i2 = i2.astype(jnp.int32)
        ca = ci >> 4
        cbi = ci & (PEER_TOPK - 1)
        e1 = jnp.zeros_like(ci)
        e2 = jnp.zeros_like(ci)
        for a in range(PEER_TOPK):
            e1 = jnp.where(ca == a, i1[a:a + 1, :], e1)
            e2 = jnp.where(cbi == a, i2[a:a + 1, :], e2)
        p = jnp.exp(sc - jnp.max(sc, axis=0, keepdims=True))
        gate = p / jnp.sum(p, axis=0, keepdims=True)
        idx_ref[h * PEER_TOPK:(h + 1) * PEER_TOPK, :] = e1 * PEER_N_KEYS + e2
        gate_ref[h * PEER_TOPK:(h + 1) * PEER_TOPK, :] = gate


def peer_topk(q, sub_keys, tt=256):
    t, width = q.shape
    assert width == PEER_HEADS * 2 * PEER_HALF
    return pl.pallas_call(
        _peer_topk_body,
        grid=(t // tt,),
        in_specs=[
            pl.BlockSpec((tt, width), lambda i: (i, 0)),
            pl.BlockSpec((PEER_N_KEYS, PEER_HALF), lambda i: (0, 0)),
            pl.BlockSpec((PEER_N_KEYS, PEER_HALF), lambda i: (0, 0)),
        ],
        out_specs=[pl.BlockSpec((PEER_SLOTS, tt), lambda i: (0, i)),
                   pl.BlockSpec((PEER_SLOTS, tt), lambda i: (0, i))],
        out_shape=[jax.ShapeDtypeStruct((PEER_SLOTS, t), jnp.int32),
                   jax.ShapeDtypeStruct((PEER_SLOTS, t), F32)],
        compiler_params=_params("arbitrary"),
        name="peer_topk",
    )(q, sub_keys[0], sub_keys[1])


def _pack_pairs(x):
    half = x.shape[1] // 2
    bits = lax.bitcast_convert_type(x.astype(BF16).astype(F32), U32)
    return (bits[:, :half] >> 16) | (bits[:, half:] & jnp.uint32(0xFFFF0000))


def _unpack_pairs(w):
    lo = lax.bitcast_convert_type(w << 16, F32)
    hi = lax.bitcast_convert_type(w & jnp.uint32(0xFFFF0000), F32)
    return lo, hi


def _pack_tables_body(u_ref, v_ref, o_ref):
    half_chunks = o_ref.shape[1] // 2
    for t, tab_ref in enumerate((u_ref, v_ref)):
        packed = _pack_pairs(tab_ref[...])
        for c in range(half_chunks):
            o_ref[:, t * half_chunks + c, 0, :] = packed[:, c * LANES:(c + 1) * LANES]


def pack_tables(u_tab, v_tab, tm=256):
    n, d = u_tab.shape
    n_chunks = d // LANES
    return pl.pallas_call(
        _pack_tables_body,
        grid=(n // tm,),
        in_specs=[pl.BlockSpec((tm, d), lambda i: (i, 0)), pl.BlockSpec((tm, d), lambda i: (i, 0))],
        out_specs=pl.BlockSpec((tm, n_chunks, 1, LANES), lambda i: (i, 0, 0, 0)),
        out_shape=jax.ShapeDtypeStruct((n, n_chunks, 1, LANES), U32),
        compiler_params=_params("arbitrary"),
        name="peer_pack_tables",
    )(u_tab, v_tab)


def _gelu(x):
    return 0.5 * x * (1.0 + lax.erf(x * (1.0 / math.sqrt(2.0))))


def _peer_mix_body(idx_hbm, x_ref, gate_ref, h_ref, tab_hbm, o_ref,
                   idx_smem, buf, mix_ref, idx_sem, row_sem, *, tb):
    step = pl.program_id(0)
    n_steps = pl.num_programs(0)
    n_win = (tb + PEER_BUFS) * PEER_SLOTS
    off = pl.multiple_of(step * (tb * PEER_SLOTS), tb * PEER_SLOTS)
    idx_copy = pltpu.make_async_copy(idx_hbm.at[pl.ds(off, n_win)], idx_smem, idx_sem)
    idx_copy.start()
    idx_copy.wait()

    def row_copy(j, slot, k):
        e = idx_smem[j * PEER_SLOTS + k]
        return pltpu.make_async_copy(tab_hbm.at[e], buf.at[slot, :, pl.ds(k, 1), :], row_sem.at[slot])

    def wait(slot):
        pltpu.make_async_copy(buf.at[slot], buf.at[slot], row_sem.at[slot]).wait()

    @pl.when(step == 0)
    def _():
        for j in range(PEER_AHEAD):
            def body(k, carry):
                row_copy(j, j, k).start()
                return carry
            lax.fori_loop(0, PEER_SLOTS, body, 0, unroll=8)

    d = x_ref.shape[1]
    half_chunks = d // (2 * LANES)
    row = lax.broadcasted_iota(jnp.int32, (PEER_SLOTS, LANES), 0)
    col = lax.broadcasted_iota(jnp.int32, (PEER_SLOTS, LANES), 1)
    eye = row == col
    per_chunk = PEER_SLOTS // (2 * half_chunks)

    def group(g, carry):
        base = pl.multiple_of(g * SUBLANES, SUBLANES)
        rows = pl.ds(base, SUBLANES)
        gate8 = gate_ref[rows, :]

        for r in range(SUBLANES):
            ahead_slot = (r + PEER_AHEAD) % PEER_BUFS
            started = [0]

            def start_some():
                for k in range(started[0], started[0] + per_chunk):
                    row_copy(base + r + PEER_AHEAD, ahead_slot, k).start(priority=k % 2)
                started[0] += per_chunk

            wait(r)
            acc = jnp.zeros((PEER_SLOTS, LANES), F32)
            for c in range(half_chunks):
                lo, hi = _unpack_pairs(buf[r, c])
                x_lo = x_ref[rows, c * LANES:(c + 1) * LANES][r:r + 1, :]
                x_hi = x_ref[rows, (half_chunks + c) * LANES:(half_chunks + c + 1) * LANES][r:r + 1, :]
                acc = acc + lo * x_lo + hi * x_hi
                start_some()
            hid = jnp.sum(acc, axis=1, keepdims=True)
            gate_row = jnp.broadcast_to(gate8[r:r + 1, :], (PEER_SLOTS, LANES))
            gate = jnp.sum(jnp.where(eye, gate_row, 0.0), axis=1, keepdims=True)
            w = jnp.broadcast_to(gate * _gelu(hid), (PEER_SLOTS, LANES))
            for c in range(half_chunks):
                lo, hi = _unpack_pairs(buf[r, half_chunks + c])
                mix_ref[r:r + 1, c * LANES:(c + 1) * LANES] = jnp.sum(lo * w, axis=0, keepdims=True)
                mix_ref[r:r + 1, (half_chunks + c) * LANES:(half_chunks + c + 1) * LANES] = (
                    jnp.sum(hi * w, axis=0, keepdims=True))
                start_some()
        o_ref[rows, :] = h_ref[rows, :] + mix_ref[...]
        return carry

    lax.fori_loop(0, tb // SUBLANES, group, 0)

    @pl.when(step == n_steps - 1)
    def _():
        for j in range(PEER_AHEAD):
            wait(j)


def peer_mix(xn, idx_t, gate_t, h, table, tb=256):
    t, d = xn.shape
    assert idx_t.shape == (PEER_SLOTS, t) and tb % PEER_BUFS == 0 and PEER_BUFS == SUBLANES
    assert PEER_AHEAD < PEER_BUFS and PEER_SLOTS % (d // LANES) == 0
    idx = jnp.pad(idx_t.T.reshape(-1), (0, PEER_BUFS * PEER_SLOTS))
    gate = gate_t.T
    n_chunks = d // LANES
    assert table.shape[1:] == (n_chunks, 1, LANES)
    blk = lambda i: (i, 0)
    return pl.pallas_call(
        functools.partial(_peer_mix_body, tb=tb),
        grid=(t // tb,),
        in_specs=[
            pl.BlockSpec(memory_space=pl.ANY),
            pl.BlockSpec((tb, d), blk),
            pl.BlockSpec((tb, PEER_SLOTS), blk),
            pl.BlockSpec((tb, d), blk),
            pl.BlockSpec(memory_space=pl.ANY),
        ],
        out_specs=pl.BlockSpec((tb, d), blk),
        out_shape=jax.ShapeDtypeStruct((t, d), F32),
        scratch_shapes=[
            pltpu.SMEM(((tb + PEER_BUFS) * PEER_SLOTS,), jnp.int32),
            pltpu.VMEM((PEER_BUFS, n_chunks, PEER_SLOTS, LANES), U32),
            pltpu.VMEM((SUBLANES, d), F32),
            pltpu.SemaphoreType.DMA(()),
            pltpu.SemaphoreType.DMA((PEER_BUFS,)),
        ],
        compiler_params=_params("arbitrary"),
        name="peer_mix",
    )(idx, xn, gate, h, table)


def peer_layer(h, norm_g, w_q, sub_keys, u_tab, v_tab):
    xn, xn_b = rmsnorm(h, [norm_g], [(0, F32), (0, BF16)])
    q = matmul(xn_b, w_q, name="peer_query")
    idx_t, gate_t = peer_topk(q, sub_keys)
    return peer_mix(xn, idx_t, gate_t, h, pack_tables(u_tab, v_tab))


def kernel(x, norm_mix, norm_ffn, norm_final, m_in_proj, m_conv_w, m_conv_b, m_dt_bias, m_a_log, m_d, m_norm,
           m_out_proj, kv_norm, w_kv, sb_w_q, sb_w_o, peer_w_q, peer_sub_keys, peer_u, peer_v):
    batch, seq, d_model = x.shape
    t = batch * seq
    h = x.reshape(t, d_model)

    n_heads = m_dt_bias.shape[1]
    proj_w = m_in_proj.shape[2] - n_heads
    u, = rmsnorm(h, [norm_mix[0]], [(0, BF16)])
    w_in = m_in_proj[0]
    zx = matmul(u, w_in, n=proj_w, name="in_proj")
    w_dt = jnp.pad(w_in[:, proj_w:], ((0, 0), (0, LANES - n_heads)))
    dtr = matmul(u, w_dt, name="in_proj_dt")
    y = mamba_ssd(zx, dtr, m_conv_w[0], m_conv_b[0], m_dt_bias[0], m_a_log[0], m_d[0], m_norm[0], batch, seq)
    h = matmul(y, m_out_proj[0], residual=h, tn=512, name="out_proj")
    h = peer_layer(h, norm_ffn[0], peer_w_q[0], peer_sub_keys[0], peer_u[0], peer_v[0])

    kvn, qn = rmsnorm(h, [kv_norm, norm_mix[1]], [(0, BF16), (1, BF16)])
    kv = matmul(kvn, w_kv, out_dtype=BF16, name="kv_proj")
    q = matmul(qn, sb_w_q[0], name="q_proj")
    o = stick_breaking_attention(q, kv, batch, seq)
    h = matmul(o, sb_w_o[0], residual=h, name="o_proj")
    h = peer_layer(h, norm_ffn[1], peer_w_q[1], peer_sub_keys[1], peer_u[1], peer_v[1])

    out, = rmsnorm(h, [norm_final], [(0, F32)])
    return out.reshape(batch, seq, d_model)
```

```python
import functools
import math

import jax
import jax.numpy as jnp
from jax import lax
from jax.experimental import pallas as pl
from jax.experimental.pallas import tpu as pltpu

F32 = jnp.float32
BF16 = jnp.bfloat16
U32 = jnp.uint32
HIGHEST = lax.Precision.HIGHEST

NORM_EPS = 1e-6
LANES = 128
SUBLANES = 8

M_HEAD_DIM = 64
M_D_STATE = 128
M_GROUPS = 8
M_D_CONV = 4
M_CHUNK = 128
M_NORM_EPS = 1e-5

SB_HEAD_DIM = 128
SB_BLOCK = 128
SB_Q_TILE = 256
SB_HEADS_PER_STEP = 8

PEER_HEADS = 8
PEER_N_KEYS = 128
PEER_TOPK = 16
PEER_HALF = 128
PEER_SLOTS = PEER_HEADS * PEER_TOPK
PEER_BUFS = 8
PEER_AHEAD = 6
PEER_LEAD = 2
PEER_RING = PEER_LEAD + 1

VMEM_LIMIT = 48 * 1024 * 1024


def _params(*sem):
    return pltpu.CompilerParams(dimension_semantics=sem, vmem_limit_bytes=VMEM_LIMIT)


def _rmsnorm_body(x_ref, g_ref, *o_refs, eps, gain_of_out):
    x = x_ref[...]
    xn = x * lax.rsqrt(jnp.mean(x * x, axis=-1, keepdims=True) + eps)
    for o_ref, gi in zip(o_refs, gain_of_out):
        o_ref[...] = (xn * g_ref[gi:gi + 1, :]).astype(o_ref.dtype)


def _split_bf16(x):
    hi = x.astype(BF16)
    return hi, (x - hi.astype(F32)).astype(BF16)


def rmsnorm(x, gains, outs, tm=256):
    t, d = x.shape
    g = jnp.stack(gains)
    blk = pl.BlockSpec((tm, d), lambda i: (i, 0))
    return pl.pallas_call(
        functools.partial(_rmsnorm_body, eps=NORM_EPS, gain_of_out=tuple(gi for gi, _ in outs)),
        grid=(t // tm,),
        in_specs=[blk, pl.BlockSpec(g.shape, lambda i: (0, 0))],
        out_specs=[blk for _ in outs],
        out_shape=[jax.ShapeDtypeStruct((t, d), dt) for _, dt in outs],
        compiler_params=_params("arbitrary"),
        name="rmsnorm",
    )(x, g)


def _mm_body(a_ref, w_ref, *rest):
    o_ref, wb_ref = rest[-2:]

    @pl.when(pl.program_id(1) == 0)
    def _():
        wb_ref[...] = w_ref[...].astype(BF16)

    acc = jnp.dot(a_ref[...], wb_ref[...], preferred_element_type=F32)
    if len(rest) == 3:
        acc = rest[0][...] + acc
    o_ref[...] = acc.astype(o_ref.dtype)


def matmul(a, w, n=None, residual=None, out_dtype=F32, tm=1024, tn=1024, by_column_block=False, name="matmul"):
    m, k = a.shape
    n = w.shape[1] if n is None else n
    tn = min(tn, n)
    assert m % tm == 0 and n % tn == 0
    in_specs = [pl.BlockSpec((tm, k), lambda j, i: (i, 0)), pl.BlockSpec((k, tn), lambda j, i: (0, j))]
    args = [a, w]
    if residual is not None:
        in_specs.append(pl.BlockSpec((tm, tn), lambda j, i: (i, j)))
        args.append(residual)
    if by_column_block:
        out_spec = pl.BlockSpec((None, tm, tn), lambda j, i: (j, i, 0))
        out_shape = (n // tn, m, tn)
    else:
        out_spec = pl.BlockSpec((tm, tn), lambda j, i: (i, j))
        out_shape = (m, n)
    return pl.pallas_call(
        _mm_body,
        grid=(n // tn, m // tm),
        in_specs=in_specs,
        out_specs=out_spec,
        out_shape=jax.ShapeDtypeStruct(out_shape, out_dtype),
        scratch_shapes=[pltpu.VMEM((k, tn), BF16)],
        compiler_params=_params("arbitrary", "arbitrary"),
        name=name,
    )(*args)


def _softplus(x):
    return jnp.maximum(x, 0.0) + jnp.log1p(jnp.exp(-jnp.abs(x)))


def _silu(x):
    return x * jax.nn.sigmoid(x)


def _causal_conv_silu(raw_ref, prev_ref, w_ref, bias_ref, out_ref):
    cur = raw_ref[...]
    prev8 = prev_ref[...]
    w = w_ref[...]
    bias = bias_ref[...]
    top = cur[0:SUBLANES, :]
    row8 = lax.broadcasted_iota(jnp.int32, top.shape, 0)
    acc = cur * w[M_D_CONV - 1:M_D_CONV, :]
    acc_top = top * w[M_D_CONV - 1:M_D_CONV, :]
    for d in range(1, M_D_CONV):
        wd = w[M_D_CONV - 1 - d:M_D_CONV - d, :]
        shifted = pltpu.roll(cur, d, 0)
        acc = acc + shifted * wd
        shifted_top = jnp.where(row8 < d, pltpu.roll(prev8, d, 0), shifted[0:SUBLANES, :])
        acc_top = acc_top + shifted_top * wd
    out_ref[...] = _silu(acc + bias)
    out_ref[0:SUBLANES, :] = _silu(acc_top + bias)
    prev_ref[...] = cur[M_CHUNK - SUBLANES:M_CHUNK, :]


def _mamba_body(z_ref, x_ref, b_ref, c_ref, dtr_ref, cwx_ref, cwb_ref, cwc_ref, cbx_ref, cbb_ref, cbc_ref,
                dtb_ref, alog_ref, dexp_ref, nw_ref, e_ref, o_ref,
                state_ref, px_ref, pb_ref, pc_ref, xc_ref, bc_ref, cc_ref, y_ref):
    @pl.when(pl.program_id(1) == 0)
    def _():
        state_ref[...] = jnp.zeros_like(state_ref)
        px_ref[...] = jnp.zeros_like(px_ref)
        pb_ref[...] = jnp.zeros_like(pb_ref)
        pc_ref[...] = jnp.zeros_like(pc_ref)

    _causal_conv_silu(x_ref, px_ref, cwx_ref, cbx_ref, xc_ref)
    _causal_conv_silu(b_ref, pb_ref, cwb_ref, cbb_ref, bc_ref)
    _causal_conv_silu(c_ref, pc_ref, cwc_ref, cbc_ref, cc_ref)

    L = M_CHUNK
    row = lax.broadcasted_iota(jnp.int32, (L, L), 0)
    col = lax.broadcasted_iota(jnp.int32, (L, L), 1)
    causal = col <= row
    ltri = causal.astype(F32)

    dt = _softplus(dtr_ref[...] + dtb_ref[...])
    dta = dt * (-jnp.exp(alog_ref[...]))
    acs = jnp.dot(ltri, dta, preferred_element_type=F32, precision=HIGHEST)
    dt_t = dt.T
    acs_t = acs.T
    e = e_ref[...]

    def expand(v):
        hi, rest = v.astype(BF16), v
        rest = rest - hi.astype(F32)
        mid = rest.astype(BF16)
        lo = (rest - mid.astype(F32)).astype(BF16)
        return (jnp.dot(hi, e, preferred_element_type=F32) + jnp.dot(mid, e, preferred_element_type=F32)
                + jnp.dot(lo, e, preferred_element_type=F32))

    dt_exp = expand(dt)
    acs_exp = expand(acs)

    gw = M_HEAD_DIM * 8
    first_half = col < M_HEAD_DIM
    for g in range(M_GROUPS):
        bg = bc_ref[:, g * M_D_STATE:(g + 1) * M_D_STATE].astype(BF16)
        cg = cc_ref[:, g * M_D_STATE:(g + 1) * M_D_STATE].astype(BF16)
        xg = xc_ref[:, g * gw:(g + 1) * gw]
        ae = acs_exp[:, g * gw:(g + 1) * gw]
        de = dt_exp[:, g * gw:(g + 1) * gw]
        cb = lax.dot_general(cg, bg, (((1,), (1,)), ((), ())), preferred_element_type=F32)
        sg = state_ref[g]
        y_off = jnp.dot(cg, sg.astype(BF16), preferred_element_type=F32) * jnp.exp(ae)
        a_last = ae[L - 1:L, :]
        to_end = jnp.exp(a_last - ae) * de
        xw = (xg * to_end).astype(BF16)
        upd = lax.dot_general(bg, xw, (((0,), (0,)), ((), ())), preferred_element_type=F32)
        state_ref[g] = sg * jnp.exp(a_last) + upd
        for jp in range(4):
            xpair = xg[:, jp * LANES:(jp + 1) * LANES]
            ys = jnp.zeros((L, LANES), F32)
            for half in range(2):
                h = g * 8 + jp * 2 + half
                a_col = jnp.sum(jnp.where(col == h, acs, 0.0), axis=1, keepdims=True)
                a_row = acs_t[h:h + 1, :]
                d_row = dt_t[h:h + 1, :]
                decay = jnp.exp(jnp.where(causal, a_col - a_row, -jnp.inf))
                w = (cb * decay * d_row).astype(BF16)
                keep = first_half if half == 0 else jnp.logical_not(first_half)
                xm = jnp.where(keep, xpair, 0.0).astype(BF16)
                ys = ys + jnp.dot(w, xm, preferred_element_type=F32)
            lo = g * gw + jp * LANES
            y_ref[:, lo:lo + LANES] = (ys + y_off[:, jp * LANES:(jp + 1) * LANES]
                                       + xpair * dexp_ref[:, lo:lo + LANES])

    for g in range(M_GROUPS):
        sl = slice(g * gw, (g + 1) * gw)
        yf = y_ref[:, sl] * _silu(z_ref[:, sl])
        ms = jnp.mean(yf * yf, axis=1, keepdims=True)
        o_ref[:, sl] = (yf * lax.rsqrt(ms + M_NORM_EPS) * nw_ref[:, sl]).astype(o_ref.dtype)


def mamba_ssd(zx, dtr, conv_w, conv_b, dt_bias, a_log, d_skip, norm_w, batch, seq):
    t = zx.shape[0]
    d_inner = norm_w.shape[0]
    n_heads = d_inner // M_HEAD_DIM
    bc_w = M_GROUPS * M_D_STATE
    nc = seq // M_CHUNK
    assert d_inner % 2048 == 0 and bc_w == 1024 and zx.shape[1] == 2 * d_inner + 2 * bc_w

    pad_h = LANES - n_heads
    dtb = jnp.pad(dt_bias, (0, pad_h)).reshape(1, LANES)
    alog = jnp.pad(a_log, (0, pad_h)).reshape(1, LANES)
    dexp = jnp.repeat(d_skip, M_HEAD_DIM).reshape(1, d_inner)
    expand = (jnp.arange(LANES)[:, None] == (jnp.arange(d_inner)[None, :] // M_HEAD_DIM)).astype(BF16)
    cwx, cwb, cwc = conv_w[:, :d_inner], conv_w[:, d_inner:d_inner + bc_w], conv_w[:, d_inner + bc_w:]
    cb2 = conv_b.reshape(1, -1)
    cbx, cbb, cbc = cb2[:, :d_inner], cb2[:, d_inner:d_inner + bc_w], cb2[:, d_inner + bc_w:]

    row_map = lambda b, c: (b * nc + c, 0)
    const = lambda b, c: (0, 0)
    xblk = d_inner // bc_w
    in_specs = [
        pl.BlockSpec((M_CHUNK, d_inner), row_map),
        pl.BlockSpec((M_CHUNK, d_inner), lambda b, c: (b * nc + c, 1)),
        pl.BlockSpec((M_CHUNK, bc_w), lambda b, c: (b * nc + c, 2 * xblk)),
        pl.BlockSpec((M_CHUNK, bc_w), lambda b, c: (b * nc + c, 2 * xblk + 1)),
        pl.BlockSpec((M_CHUNK, LANES), row_map),
        pl.BlockSpec((M_D_CONV, d_inner), const), pl.BlockSpec((M_D_CONV, bc_w), const),
        pl.BlockSpec((M_D_CONV, bc_w), const),
        pl.BlockSpec((1, d_inner), const), pl.BlockSpec((1, bc_w), const), pl.BlockSpec((1, bc_w), const),
        pl.BlockSpec((1, LANES), const), pl.BlockSpec((1, LANES), const),
        pl.BlockSpec((1, d_inner), const), pl.BlockSpec((1, d_inner), const),
        pl.BlockSpec((LANES, d_inner), const),
    ]
    return pl.pallas_call(
        _mamba_body,
        grid=(batch, nc),
        in_specs=in_specs,
        out_specs=pl.BlockSpec((M_CHUNK, d_inner), row_map),
        out_shape=jax.ShapeDtypeStruct((t, d_inner), BF16),
        scratch_shapes=[
            pltpu.VMEM((M_GROUPS, M_D_STATE, M_HEAD_DIM * 8), F32),
            pltpu.VMEM((SUBLANES, d_inner), F32), pltpu.VMEM((SUBLANES, bc_w), F32),
            pltpu.VMEM((SUBLANES, bc_w), F32),
            pltpu.VMEM((M_CHUNK, d_inner), F32), pltpu.VMEM((M_CHUNK, bc_w), F32),
            pltpu.VMEM((M_CHUNK, bc_w), F32),
            pltpu.VMEM((M_CHUNK, d_inner), F32),
        ],
        compiler_params=_params("arbitrary", "arbitrary"),
        name="mamba_ssd",
    )(zx, zx, zx, zx, dtr, cwx, cwb, cwc, cbx, cbb, cbc, dtb, alog, dexp, norm_w.reshape(1, d_inner), expand)


def _attn_body(q_ref, k_ref, v_ref, o_ref, *scratch, scale):
    acc_refs = scratch[:SB_HEADS_PER_STEP]
    tail_refs = scratch[SB_HEADS_PER_STEP:]
    qt = pl.program_id(2)
    tq, blk, hd = SB_Q_TILE, SB_BLOCK, SB_HEAD_DIM
    per_tile = tq // blk
    row = lax.broadcasted_iota(jnp.int32, (tq, blk), 0)
    col = lax.broadcasted_iota(jnp.int32, (tq, blk), 1)
    r2 = lax.broadcasted_iota(jnp.int32, (blk, 2 * blk), 0)
    c2 = lax.broadcasted_iota(jnp.int32, (blk, 2 * blk), 1)
    sums = ((r2 > c2) | (c2 >= blk)).astype(BF16)
    for ref in scratch:
        ref[...] = jnp.zeros_like(ref)

    def make_step(first_kb, masked):
        def step(i, carry):
            kb = first_kb - i
            start = pl.multiple_of(kb * blk, blk)
            if masked:
                strict = (kb * blk + col) < (qt * tq + row)
            heads = range(SB_HEADS_PER_STEP)
            lanes = [slice(h * hd, (h + 1) * hd) for h in heads]
            z = [lax.dot_general((q_ref[:, lanes[h]] * scale).astype(BF16), k_ref[pl.ds(start, blk), lanes[h]],
                                 (((1,), (1,)), ((), ())), preferred_element_type=F32) for h in heads]
            log_1mb, log_b, s = [], [], []
            for h in heads:
                lp = jnp.log(1.0 + jnp.exp(-jnp.abs(z[h])))
                l1 = -(jnp.maximum(z[h], 0.0) + lp)
                log_b.append(z[h] + l1)
                log_1mb.append(jnp.where(strict, l1, 0.0) if masked else l1)
            for h in heads:
                hi, lo = _split_bf16(log_1mb[h])
                s.append(jnp.dot(hi, sums, preferred_element_type=F32)
                         + jnp.dot(lo, sums, preferred_element_type=F32))
            for h in heads:
                tail = tail_refs[h][...]
                a = jnp.exp(log_b[h] + s[h][:, :blk] + tail)
                if masked:
                    a = jnp.where(strict, a, 0.0)
                acc_refs[h][...] += jnp.dot(a.astype(BF16), v_ref[pl.ds(start, blk), lanes[h]],
                                            preferred_element_type=F32)
                tail_refs[h][...] = tail + s[h][:, blk:]
            return carry
        return step

    lax.fori_loop(0, per_tile, make_step(qt * per_tile + per_tile - 1, True), 0)
    lax.fori_loop(0, qt * per_tile, make_step(qt * per_tile - 1, False), 0)
    for h in range(SB_HEADS_PER_STEP):
        o_ref[:, h * hd:(h + 1) * hd] = acc_refs[h][...].astype(o_ref.dtype)


def stick_breaking_attention(q, kv, batch, seq):
    t, width = q.shape
    nq = seq // SB_Q_TILE
    gw = SB_HEADS_PER_STEP * SB_HEAD_DIM
    n_groups = width // gw
    return pl.pallas_call(
        functools.partial(_attn_body, scale=1.0 / math.sqrt(SB_HEAD_DIM)),
        grid=(batch, n_groups, nq),
        in_specs=[
            pl.BlockSpec((SB_Q_TILE, gw), lambda b, g, i: (b * nq + i, g)),
            pl.BlockSpec((seq, gw), lambda b, g, i: (b, g)),
            pl.BlockSpec((seq, gw), lambda b, g, i: (b, n_groups + g)),
        ],
        out_specs=pl.BlockSpec((SB_Q_TILE, gw), lambda b, g, i: (b * nq + i, g)),
        out_shape=jax.ShapeDtypeStruct((t, width), BF16),
        scratch_shapes=([pltpu.VMEM((SB_Q_TILE, SB_HEAD_DIM), F32)] * SB_HEADS_PER_STEP
                        + [pltpu.VMEM((SB_Q_TILE, SB_BLOCK), F32)] * SB_HEADS_PER_STEP),
        compiler_params=_params("arbitrary", "arbitrary", "arbitrary"),
        name="stick_breaking_attention",
    )(q, kv, kv)


def _top16_rows(s, ids):
    tokens = s.shape[1]
    slot = lax.broadcasted_iota(jnp.int32, (PEER_TOPK, tokens), 0)
    vals = jnp.zeros((PEER_TOPK, tokens), F32)
    wins = jnp.zeros((PEER_TOPK, tokens), F32)
    for r in range(PEER_TOPK):
        m = jnp.max(s, axis=0, keepdims=True)
        first = jnp.min(jnp.where(s == m, ids, jnp.inf), axis=0, keepdims=True)
        s = jnp.where(ids == first, -jnp.inf, s)
        vals = jnp.where(slot == r, m, vals)
        wins = jnp.where(slot == r, first, wins)
    return vals, wins


def _pair_candidates(v1, v2):
    k = PEER_TOPK
    tokens = v1.shape[1]
    sub = lax.broadcasted_iota(jnp.int32, (SUBLANES, tokens), 0)
    sub16 = lax.broadcasted_iota(jnp.int32, (k, tokens), 0)
    vals, ids = [], []
    vals.append(v1[0:1, :] + v2)
    ids.append(sub16)
    vals.append(jnp.where(sub16 >= 1, v1 + v2[0:1, :], -jnp.inf))
    ids.append(sub16 * k)
    for a in range(1, 5):
        b_max = k // (a + 1) - 1
        vals.append(jnp.where((sub >= 1) & (sub <= b_max), v1[a:a + 1, :] + v2[0:SUBLANES, :], -jnp.inf))
        ids.append(sub + a * k)
    vals.append(jnp.where((sub >= 5) & (sub <= 7), v1[0:SUBLANES, :] + v2[1:2, :], -jnp.inf))
    ids.append(sub * k + 1)
    return jnp.concatenate(vals, axis=0), jnp.concatenate(ids, axis=0).astype(F32)


def _route_scores(head, qh_ref, k1_ref, k2_ref, s_ref):
    nt = (((1,), (1,)), ((), ()))
    q = qh_ref[head]
    s_ref[0] = lax.dot_general(k1_ref[...], q[:, :PEER_HALF], nt, preferred_element_type=F32,
                               precision=HIGHEST)
    s_ref[1] = lax.dot_general(k2_ref[...], q[:, PEER_HALF:], nt, preferred_element_type=F32,
                               precision=HIGHEST)


def _route_top_keys(which, s_ref, v_ref, i_ref):
    s = s_ref[which]
    key_ids = lax.broadcasted_iota(jnp.int32, s.shape, 0).astype(F32)
    v_ref[which], i_ref[which] = _top16_rows(s, key_ids)


def _route_pairs(head, slot, v_ref, i_ref, idx_t_ref, gate_t_ref):
    sc, ci = _top16_rows(*_pair_candidates(v_ref[0], v_ref[1]))
    ci = ci.astype(jnp.int32)
    i1 = i_ref[0].astype(jnp.int32)
    i2 = i_ref[1].astype(jnp.int32)
    ca = ci >> 4
    cbi = ci & (PEER_TOPK - 1)
    e1 = jnp.zeros_like(ci)
    e2 = jnp.zeros_like(ci)
    for a in range(PEER_TOPK):
        e1 = jnp.where(ca == a, i1[a:a + 1, :], e1)
        e2 = jnp.where(cbi == a, i2[a:a + 1, :], e2)
    p = jnp.exp(sc - jnp.max(sc, axis=0, keepdims=True))
    gate = p / jnp.sum(p, axis=0, keepdims=True)
    rows = pl.ds(pl.multiple_of(head * PEER_TOPK, PEER_TOPK), PEER_TOPK)
    idx_t_ref[slot, rows, :] = e1 * PEER_N_KEYS + e2
    gate_t_ref[slot, rows, :] = gate


ROUTE_PIECES = 4 * PEER_HEADS


def _pack_pairs(x):
    half = x.shape[1] // 2
    bits = lax.bitcast_convert_type(x.astype(BF16).astype(F32), U32)
    return (bits[:, :half] >> 16) | (bits[:, half:] & jnp.uint32(0xFFFF0000))


def _unpack_pairs(w):
    lo = lax.bitcast_convert_type(w << 16, F32)
    hi = lax.bitcast_convert_type(w & jnp.uint32(0xFFFF0000), F32)
    return lo, hi


def _pack_tables_body(u_ref, v_ref, o_ref):
    half_chunks = o_ref.shape[1] // 2
    for t, tab_ref in enumerate((u_ref, v_ref)):
        packed = _pack_pairs(tab_ref[...])
        for c in range(half_chunks):
            o_ref[:, t * half_chunks + c, 0, :] = packed[:, c * LANES:(c + 1) * LANES]


def pack_tables(u_tabs, v_tabs, layer, tm=256):
    _, n, d = u_tabs.shape
    n_chunks = d // LANES
    layer_rows = pl.BlockSpec((None, tm, d), lambda i: (layer, i, 0))
    return pl.pallas_call(
        _pack_tables_body,
        grid=(n // tm,),
        in_specs=[layer_rows, layer_rows],
        out_specs=pl.BlockSpec((tm, n_chunks, 1, LANES), lambda i: (i, 0, 0, 0)),
        out_shape=jax.ShapeDtypeStruct((n, n_chunks, 1, LANES), U32),
        compiler_params=_params("arbitrary"),
        name="peer_pack_tables",
    )(u_tabs, v_tabs)


def _gelu(x):
    return 0.5 * x * (1.0 + lax.erf(x * (1.0 / math.sqrt(2.0))))


def _peer_body(qh_ref, k1_ref, k2_ref, x_ref, h_ref, tab_hbm, o_ref,
               s_ref, v_ref, i_ref, idx_t_ref, gate_t_ref, idx_rows, gate_rows,
               idx_smem, buf, mix_ref, idx_sem, row_sem, *, tb, n_blocks):
    step = pl.program_id(0)
    blk = step - PEER_LEAD
    mixing = blk >= 0
    n_groups = tb // SUBLANES
    pieces_per_group = ROUTE_PIECES // n_groups

    @pl.when(step == 0)
    def _():
        idx_t_ref[...] = jnp.zeros_like(idx_t_ref)

    def route_piece(p):
        head = p // 4
        slot = lax.rem(step, PEER_RING)
        lax.switch(p % 4, [
            lambda hd: _route_scores(hd, qh_ref, k1_ref, k2_ref, s_ref),
            lambda hd: _route_top_keys(0, s_ref, v_ref, i_ref),
            lambda hd: _route_top_keys(1, s_ref, v_ref, i_ref),
            lambda hd: _route_pairs(hd, slot, v_ref, i_ref, idx_t_ref, gate_t_ref),
        ], head)

    def row_copy(j, slot, k):
        e = idx_smem[j, k]
        return pltpu.make_async_copy(tab_hbm.at[e], buf.at[slot, :, pl.ds(k, 1), :], row_sem.at[slot])

    def wait(slot):
        pltpu.make_async_copy(buf.at[slot], buf.at[slot], row_sem.at[slot]).wait()

    @pl.when(mixing)
    def _():
        slot = lax.rem(blk, PEER_RING)
        nxt = lax.rem(blk + 1, PEER_RING)
        for j in range(tb // LANES):
            cols = slice(j * LANES, (j + 1) * LANES)
            gate_rows[cols, :] = gate_t_ref[slot, :, cols].T
            idx_rows[cols, :] = idx_t_ref[slot, :, cols].astype(F32).T.astype(jnp.int32)
        head_rows = idx_t_ref[nxt, :, 0:LANES].astype(F32).T.astype(jnp.int32)
        idx_rows[tb:tb + PEER_BUFS, :] = head_rows[0:PEER_BUFS, :]
        ids = pltpu.make_async_copy(idx_rows, idx_smem, idx_sem)
        ids.start()
        ids.wait()

    @pl.when(blk == 0)
    def _():
        for j in range(PEER_AHEAD):
            def body(k, carry):
                row_copy(j, j, k).start()
                return carry
            lax.fori_loop(0, PEER_SLOTS, body, 0, unroll=8)

    d = x_ref.shape[1]
    half_chunks = d // (2 * LANES)
    row = lax.broadcasted_iota(jnp.int32, (PEER_SLOTS, LANES), 0)
    col = lax.broadcasted_iota(jnp.int32, (PEER_SLOTS, LANES), 1)
    eye = row == col
    per_chunk = PEER_SLOTS // (2 * half_chunks)

    def mix_group(g):
        base = pl.multiple_of(g * SUBLANES, SUBLANES)
        rows = pl.ds(base, SUBLANES)
        gate8 = gate_rows[rows, :]

        for r in range(SUBLANES):
            ahead_slot = (r + PEER_AHEAD) % PEER_BUFS
            started = [0]

            def start_some():
                for k in range(started[0], started[0] + per_chunk):
                    row_copy(base + r + PEER_AHEAD, ahead_slot, k).start(priority=k % 2)
                started[0] += per_chunk

            wait(r)
            acc = jnp.zeros((PEER_SLOTS, LANES), F32)
            for c in range(half_chunks):
                lo, hi = _unpack_pairs(buf[r, c])
                x_lo = x_ref[rows, c * LANES:(c + 1) * LANES][r:r + 1, :]
                x_hi = x_ref[rows, (half_chunks + c) * LANES:(half_chunks + c + 1) * LANES][r:r + 1, :]
                acc = acc + lo * x_lo + hi * x_hi
                start_some()
            hid = jnp.sum(acc, axis=1, keepdims=True)
            gate_row = jnp.broadcast_to(gate8[r:r + 1, :], (PEER_SLOTS, LANES))
            gate = jnp.sum(jnp.where(eye, gate_row, 0.0), axis=1, keepdims=True)
            w = jnp.broadcast_to(gate * _gelu(hid), (PEER_SLOTS, LANES))
            for c in range(half_chunks):
                lo, hi = _unpack_pairs(buf[r, half_chunks + c])
                mix_ref[r:r + 1, c * LANES:(c + 1) * LANES] = jnp.sum(lo * w, axis=0, keepdims=True)
                mix_ref[r:r + 1, (half_chunks + c) * LANES:(half_chunks + c + 1) * LANES] = (
                    jnp.sum(hi * w, axis=0, keepdims=True))
                start_some()
        o_ref[rows, :] = h_ref[rows, :] + mix_ref[...]

    def group(g, carry):
        @pl.when(mixing)
        def _():
            mix_group(g)

        @pl.when(step < n_blocks)
        def _():
            for p in range(pieces_per_group):
                route_piece(g * pieces_per_group + p)
        return carry

    lax.fori_loop(0, n_groups, group, 0)

    @pl.when(blk == n_blocks - 1)
    def _():
        for j in range(PEER_AHEAD):
            wait(j)


def peer_route_and_mix(xn, qh, sub_keys, h, table, tb=256):
    t, d = xn.shape
    n_blocks = t // tb
    n_chunks = d // LANES
    assert PEER_BUFS == SUBLANES and PEER_AHEAD < PEER_BUFS and PEER_SLOTS % n_chunks == 0
    assert tb % LANES == 0 and ROUTE_PIECES % (tb // SUBLANES) == 0 and n_blocks * tb == t
    assert table.shape[1:] == (n_chunks, 1, LANES) and qh.shape == (PEER_HEADS, t, 2 * PEER_HALF)
    mixed = lambda s: (jnp.maximum(s - PEER_LEAD, 0), 0)
    keys = pl.BlockSpec((PEER_N_KEYS, PEER_HALF), lambda s: (0, 0))
    return pl.pallas_call(
        functools.partial(_peer_body, tb=tb, n_blocks=n_blocks),
        grid=(n_blocks + PEER_LEAD,),
        in_specs=[
            pl.BlockSpec((PEER_HEADS, tb, 2 * PEER_HALF), lambda s: (0, jnp.minimum(s, n_blocks - 1), 0)),
            keys, keys,
            pl.BlockSpec((tb, d), mixed),
            pl.BlockSpec((tb, d), mixed),
            pl.BlockSpec(memory_space=pl.ANY),
        ],
        out_specs=pl.BlockSpec((tb, d), mixed),
        out_shape=jax.ShapeDtypeStruct((t, d), F32),
        scratch_shapes=[
            pltpu.VMEM((2, PEER_N_KEYS, tb), F32),
            pltpu.VMEM((2, PEER_TOPK, tb), F32), pltpu.VMEM((2, PEER_TOPK, tb), F32),
            pltpu.VMEM((PEER_RING, PEER_SLOTS, tb), jnp.int32),
            pltpu.VMEM((PEER_RING, PEER_SLOTS, tb), F32),
            pltpu.VMEM((tb + PEER_BUFS, PEER_SLOTS), jnp.int32),
            pltpu.VMEM((tb, PEER_SLOTS), F32),
            pltpu.SMEM((tb + PEER_BUFS, PEER_SLOTS), jnp.int32),
            pltpu.VMEM((PEER_BUFS, n_chunks, PEER_SLOTS, LANES), U32),
            pltpu.VMEM((SUBLANES, d), F32),
            pltpu.SemaphoreType.DMA(()),
            pltpu.SemaphoreType.DMA((PEER_BUFS,)),
        ],
        compiler_params=_params("arbitrary"),
        name="peer_route_mix",
    )(qh, sub_keys[0], sub_keys[1], xn, h, table)


def peer_layer(h, norm_g, w_q, sub_keys, u_tabs, v_tabs, layer):
    xn, xn_b = rmsnorm(h, [norm_g], [(0, F32), (0, BF16)])
    qh = matmul(xn_b, w_q, tn=2 * PEER_HALF, by_column_block=True, name="peer_query")
    return peer_route_and_mix(xn, qh, sub_keys, h, pack_tables(u_tabs, v_tabs, layer))


def kernel(x, norm_mix, norm_ffn, norm_final, m_in_proj, m_conv_w, m_conv_b, m_dt_bias, m_a_log, m_d, m_norm,
           m_out_proj, kv_norm, w_kv, sb_w_q, sb_w_o, peer_w_q, peer_sub_keys, peer_u, peer_v):
    batch, seq, d_model = x.shape
    t = batch * seq
    h = x.reshape(t, d_model)

    n_heads = m_dt_bias.shape[1]
    proj_w = m_in_proj.shape[2] - n_heads
    u, = rmsnorm(h, [norm_mix[0]], [(0, BF16)])
    w_in = m_in_proj[0]
    zx = matmul(u, w_in, n=proj_w, name="in_proj")
    w_dt = jnp.pad(w_in[:, proj_w:], ((0, 0), (0, LANES - n_heads)))
    dtr = matmul(u, w_dt, name="in_proj_dt")
    y = mamba_ssd(zx, dtr, m_conv_w[0], m_conv_b[0], m_dt_bias[0], m_a_log[0], m_d[0], m_norm[0], batch, seq)
    h = matmul(y, m_out_proj[0], residual=h, tn=512, name="out_proj")
    h = peer_layer(h, norm_ffn[0], peer_w_q[0], peer_sub_keys[0], peer_u, peer_v, 0)

    kvn, qn = rmsnorm(h, [kv_norm, norm_mix[1]], [(0, BF16), (1, BF16)])
    kv = matmul(kvn, w_kv, out_dtype=BF16, name="kv_proj")
    q = matmul(qn, sb_w_q[0], name="q_proj")
    o = stick_breaking_attention(q, kv, batch, seq)
    h = matmul(o, sb_w_o[0], residual=h, name="o_proj")
    h = peer_layer(h, norm_ffn[1], peer_w_q[1], peer_sub_keys[1], peer_u, peer_v, 1)

    out, = rmsnorm(h, [norm_final], [(0, F32)])
    return out.reshape(batch, seq, d_model)
```

```python
import functools
import math

import jax
import jax.numpy as jnp
from jax import lax
from jax.experimental import pallas as pl
from jax.experimental.pallas import tpu as pltpu

F32 = jnp.float32
BF16 = jnp.bfloat16
U32 = jnp.uint32
HIGHEST = lax.Precision.HIGHEST

NORM_EPS = 1e-6
LANES = 128
SUBLANES = 8

M_HEAD_DIM = 64
M_D_STATE = 128
M_GROUPS = 8
M_D_CONV = 4
M_CHUNK = 128
M_NORM_EPS = 1e-5

SB_HEAD_DIM = 128
SB_BLOCK = 128
SB_Q_TILE = 256
SB_HEADS_PER_STEP = 8

PEER_HEADS = 8
PEER_N_KEYS = 128
PEER_TOPK = 16
PEER_HALF = 128
PEER_SLOTS = PEER_HEADS * PEER_TOPK
PEER_BUFS = 8
PEER_AHEAD = 6
PEER_LEAD = 2
PEER_RING = PEER_LEAD + 1

VMEM_LIMIT = 48 * 1024 * 1024


def _params(*sem):
    return pltpu.CompilerParams(dimension_semantics=sem, vmem_limit_bytes=VMEM_LIMIT)


def _rmsnorm_body(x_ref, g_ref, *o_refs, eps, gain_of_out):
    x = x_ref[...]
    xn = x * lax.rsqrt(jnp.mean(x * x, axis=-1, keepdims=True) + eps)
    for o_ref, gi in zip(o_refs, gain_of_out):
        o_ref[...] = (xn * g_ref[gi:gi + 1, :]).astype(o_ref.dtype)


def _split_bf16(x):
    hi = x.astype(BF16)
    return hi, (x - hi.astype(F32)).astype(BF16)


def rmsnorm(x, gains, outs, tm=256):
    t, d = x.shape
    g = jnp.stack(gains)
    blk = pl.BlockSpec((tm, d), lambda i: (i, 0))
    return pl.pallas_call(
        functools.partial(_rmsnorm_body, eps=NORM_EPS, gain_of_out=tuple(gi for gi, _ in outs)),
        grid=(t // tm,),
        in_specs=[blk, pl.BlockSpec(g.shape, lambda i: (0, 0))],
        out_specs=[blk for _ in outs],
        out_shape=[jax.ShapeDtypeStruct((t, d), dt) for _, dt in outs],
        compiler_params=_params("arbitrary"),
        name="rmsnorm",
    )(x, g)


def _mm_body(a_ref, w_ref, *rest):
    o_ref, wb_ref = rest[-2:]

    @pl.when(pl.program_id(1) == 0)
    def _():
        wb_ref[...] = w_ref[...].astype(BF16)

    acc = jnp.dot(a_ref[...], wb_ref[...], preferred_element_type=F32)
    if len(rest) == 3:
        acc = rest[0][...] + acc
    o_ref[...] = acc.astype(o_ref.dtype)


def matmul(a, w, n=None, residual=None, out_dtype=F32, tm=1024, tn=1024, by_column_block=False, name="matmul"):
    m, k = a.shape
    n = w.shape[1] if n is None else n
    tn = min(tn, n)
    assert m % tm == 0 and n % tn == 0
    in_specs = [pl.BlockSpec((tm, k), lambda j, i: (i, 0)), pl.BlockSpec((k, tn), lambda j, i: (0, j))]
    args = [a, w]
    if residual is not None:
        in_specs.append(pl.BlockSpec((tm, tn), lambda j, i: (i, j)))
        args.append(residual)
    if by_column_block:
        out_spec = pl.BlockSpec((None, tm, tn), lambda j, i: (j, i, 0))
        out_shape = (n // tn, m, tn)
    else:
        out_spec = pl.BlockSpec((tm, tn), lambda j, i: (i, j))
        out_shape = (m, n)
    return pl.pallas_call(
        _mm_body,
        grid=(n // tn, m // tm),
        in_specs=in_specs,
        out_specs=out_spec,
        out_shape=jax.ShapeDtypeStruct(out_shape, out_dtype),
        scratch_shapes=[pltpu.VMEM((k, tn), BF16)],
        compiler_params=_params("arbitrary", "arbitrary"),
        name=name,
    )(*args)


def _softplus(x):
    return jnp.maximum(x, 0.0) + jnp.log1p(jnp.exp(-jnp.abs(x)))


def _silu(x):
    return x * jax.nn.sigmoid(x)


def _causal_conv_silu(raw_ref, prev_ref, w_ref, bias_ref, out_ref):
    cur = raw_ref[...]
    prev8 = prev_ref[...]
    w = w_ref[...]
    bias = bias_ref[...]
    top = cur[0:SUBLANES, :]
    row8 = lax.broadcasted_iota(jnp.int32, top.shape, 0)
    acc = cur * w[M_D_CONV - 1:M_D_CONV, :]
    acc_top = top * w[M_D_CONV - 1:M_D_CONV, :]
    for d in range(1, M_D_CONV):
        wd = w[M_D_CONV - 1 - d:M_D_CONV - d, :]
        shifted = pltpu.roll(cur, d, 0)
        acc = acc + shifted * wd
        shifted_top = jnp.where(row8 < d, pltpu.roll(prev8, d, 0), shifted[0:SUBLANES, :])
        acc_top = acc_top + shifted_top * wd
    out_ref[...] = _silu(acc + bias)
    out_ref[0:SUBLANES, :] = _silu(acc_top + bias)
    prev_ref[...] = cur[M_CHUNK - SUBLANES:M_CHUNK, :]


def _mamba_body(z_ref, x_ref, b_ref, c_ref, dtr_ref, cwx_ref, cwb_ref, cwc_ref, cbx_ref, cbb_ref, cbc_ref,
                dtb_ref, alog_ref, dexp_ref, nw_ref, e_ref, o_ref,
                state_ref, px_ref, pb_ref, pc_ref, xc_ref, bc_ref, cc_ref, y_ref):
    @pl.when(pl.program_id(1) == 0)
    def _():
        state_ref[...] = jnp.zeros_like(state_ref)
        px_ref[...] = jnp.zeros_like(px_ref)
        pb_ref[...] = jnp.zeros_like(pb_ref)
        pc_ref[...] = jnp.zeros_like(pc_ref)

    _causal_conv_silu(x_ref, px_ref, cwx_ref, cbx_ref, xc_ref)
    _causal_conv_silu(b_ref, pb_ref, cwb_ref, cbb_ref, bc_ref)
    _causal_conv_silu(c_ref, pc_ref, cwc_ref, cbc_ref, cc_ref)

    L = M_CHUNK
    row = lax.broadcasted_iota(jnp.int32, (L, L), 0)
    col = lax.broadcasted_iota(jnp.int32, (L, L), 1)
    causal = col <= row
    ltri = causal.astype(F32)

    dt = _softplus(dtr_ref[...] + dtb_ref[...])
    dta = dt * (-jnp.exp(alog_ref[...]))
    acs = jnp.dot(ltri, dta, preferred_element_type=F32, precision=HIGHEST)
    dt_t = dt.T
    acs_t = acs.T
    e = e_ref[...]

    def expand(v):
        hi, rest = v.astype(BF16), v
        rest = rest - hi.astype(F32)
        mid = rest.astype(BF16)
        lo = (rest - mid.astype(F32)).astype(BF16)
        return (jnp.dot(hi, e, preferred_element_type=F32) + jnp.dot(mid, e, preferred_element_type=F32)
                + jnp.dot(lo, e, preferred_element_type=F32))

    dt_exp = expand(dt)
    acs_exp = expand(acs)

    gw = M_HEAD_DIM * 8
    first_half = col < M_HEAD_DIM
    for g in range(M_GROUPS):
        bg = bc_ref[:, g * M_D_STATE:(g + 1) * M_D_STATE].astype(BF16)
        cg = cc_ref[:, g * M_D_STATE:(g + 1) * M_D_STATE].astype(BF16)
        xg = xc_ref[:, g * gw:(g + 1) * gw]
        ae = acs_exp[:, g * gw:(g + 1) * gw]
        de = dt_exp[:, g * gw:(g + 1) * gw]
        cb = lax.dot_general(cg, bg, (((1,), (1,)), ((), ())), preferred_element_type=F32)
        sg = state_ref[g]
        y_off = jnp.dot(cg, sg.astype(BF16), preferred_element_type=F32) * jnp.exp(ae)
        a_last = ae[L - 1:L, :]
        to_end = jnp.exp(a_last - ae) * de
        xw = (xg * to_end).astype(BF16)
        upd = lax.dot_general(bg, xw, (((0,), (0,)), ((), ())), preferred_element_type=F32)
        state_ref[g] = sg * jnp.exp(a_last) + upd
        for jp in range(4):
            xpair = xg[:, jp * LANES:(jp + 1) * LANES]
            ys = jnp.zeros((L, LANES), F32)
            for half in range(2):
                h = g * 8 + jp * 2 + half
                a_col = jnp.sum(jnp.where(col == h, acs, 0.0), axis=1, keepdims=True)
                a_row = acs_t[h:h + 1, :]
                d_row = dt_t[h:h + 1, :]
                decay = jnp.exp(jnp.where(causal, a_col - a_row, -jnp.inf))
                w = (cb * decay * d_row).astype(BF16)
                keep = first_half if half == 0 else jnp.logical_not(first_half)
                xm = jnp.where(keep, xpair, 0.0).astype(BF16)
                ys = ys + jnp.dot(w, xm, preferred_element_type=F32)
            lo = g * gw + jp * LANES
            y_ref[:, lo:lo + LANES] = (ys + y_off[:, jp * LANES:(jp + 1) * LANES]
                                       + xpair * dexp_ref[:, lo:lo + LANES])

    for g in range(M_GROUPS):
        sl = slice(g * gw, (g + 1) * gw)
        yf = y_ref[:, sl] * _silu(z_ref[:, sl])
        ms = jnp.mean(yf * yf, axis=1, keepdims=True)
        o_ref[:, sl] = (yf * lax.rsqrt(ms + M_NORM_EPS) * nw_ref[:, sl]).astype(o_ref.dtype)


def mamba_ssd(zx, dtr, conv_w, conv_b, dt_bias, a_log, d_skip, norm_w, batch, seq):
    t = zx.shape[0]
    d_inner = norm_w.shape[0]
    n_heads = d_inner // M_HEAD_DIM
    bc_w = M_GROUPS * M_D_STATE
    nc = seq // M_CHUNK
    assert d_inner % 2048 == 0 and bc_w == 1024 and zx.shape[1] == 2 * d_inner + 2 * bc_w

    pad_h = LANES - n_heads
    dtb = jnp.pad(dt_bias, (0, pad_h)).reshape(1, LANES)
    alog = jnp.pad(a_log, (0, pad_h)).reshape(1, LANES)
    dexp = jnp.repeat(d_skip, M_HEAD_DIM).reshape(1, d_inner)
    expand = (jnp.arange(LANES)[:, None] == (jnp.arange(d_inner)[None, :] // M_HEAD_DIM)).astype(BF16)
    cwx, cwb, cwc = conv_w[:, :d_inner], conv_w[:, d_inner:d_inner + bc_w], conv_w[:, d_inner + bc_w:]
    cb2 = conv_b.reshape(1, -1)
    cbx, cbb, cbc = cb2[:, :d_inner], cb2[:, d_inner:d_inner + bc_w], cb2[:, d_inner + bc_w:]

    row_map = lambda b, c: (b * nc + c, 0)
    const = lambda b, c: (0, 0)
    xblk = d_inner // bc_w
    in_specs = [
        pl.BlockSpec((M_CHUNK, d_inner), row_map),
        pl.BlockSpec((M_CHUNK, d_inner), lambda b, c: (b * nc + c, 1)),
        pl.BlockSpec((M_CHUNK, bc_w), lambda b, c: (b * nc + c, 2 * xblk)),
        pl.BlockSpec((M_CHUNK, bc_w), lambda b, c: (b * nc + c, 2 * xblk + 1)),
        pl.BlockSpec((M_CHUNK, LANES), row_map),
        pl.BlockSpec((M_D_CONV, d_inner), const), pl.BlockSpec((M_D_CONV, bc_w), const),
        pl.BlockSpec((M_D_CONV, bc_w), const),
        pl.BlockSpec((1, d_inner), const), pl.BlockSpec((1, bc_w), const), pl.BlockSpec((1, bc_w), const),
        pl.BlockSpec((1, LANES), const), pl.BlockSpec((1, LANES), const),
        pl.BlockSpec((1, d_inner), const), pl.BlockSpec((1, d_inner), const),
        pl.BlockSpec((LANES, d_inner), const),
    ]
    return pl.pallas_call(
        _mamba_body,
        grid=(batch, nc),
        in_specs=in_specs,
        out_specs=pl.BlockSpec((M_CHUNK, d_inner), row_map),
        out_shape=jax.ShapeDtypeStruct((t, d_inner), BF16),
        scratch_shapes=[
            pltpu.VMEM((M_GROUPS, M_D_STATE, M_HEAD_DIM * 8), F32),
            pltpu.VMEM((SUBLANES, d_inner), F32), pltpu.VMEM((SUBLANES, bc_w), F32),
            pltpu.VMEM((SUBLANES, bc_w), F32),
            pltpu.VMEM((M_CHUNK, d_inner), F32), pltpu.VMEM((M_CHUNK, bc_w), F32),
            pltpu.VMEM((M_CHUNK, bc_w), F32),
            pltpu.VMEM((M_CHUNK, d_inner), F32),
        ],
        compiler_params=_params("arbitrary", "arbitrary"),
        name="mamba_ssd",
    )(zx, zx, zx, zx, dtr, cwx, cwb, cwc, cbx, cbb, cbc, dtb, alog, dexp, norm_w.reshape(1, d_inner), expand)


def _attn_body(q_ref, k_ref, v_ref, o_ref, *scratch, scale):
    acc_refs = scratch[:SB_HEADS_PER_STEP]
    tail_refs = scratch[SB_HEADS_PER_STEP:]
    qt = pl.program_id(2)
    tq, blk, hd = SB_Q_TILE, SB_BLOCK, SB_HEAD_DIM
    per_tile = tq // blk
    row = lax.broadcasted_iota(jnp.int32, (tq, blk), 0)
    col = lax.broadcasted_iota(jnp.int32, (tq, blk), 1)
    r2 = lax.broadcasted_iota(jnp.int32, (blk, 2 * blk), 0)
    c2 = lax.broadcasted_iota(jnp.int32, (blk, 2 * blk), 1)
    sums = ((r2 > c2) | (c2 >= blk)).astype(BF16)
    for ref in scratch:
        ref[...] = jnp.zeros_like(ref)

    def make_step(first_kb, masked):
        def step(i, carry):
            kb = first_kb - i
            start = pl.multiple_of(kb * blk, blk)
            if masked:
                strict = (kb * blk + col) < (qt * tq + row)
            heads = range(SB_HEADS_PER_STEP)
            lanes = [slice(h * hd, (h + 1) * hd) for h in heads]
            z = [lax.dot_general((q_ref[:, lanes[h]] * scale).astype(BF16), k_ref[pl.ds(start, blk), lanes[h]],
                                 (((1,), (1,)), ((), ())), preferred_element_type=F32) for h in heads]
            log_1mb, log_b, s = [], [], []
            for h in heads:
                lp = jnp.log(1.0 + jnp.exp(-jnp.abs(z[h])))
                l1 = -(jnp.maximum(z[h], 0.0) + lp)
                log_b.append(z[h] + l1)
                log_1mb.append(jnp.where(strict, l1, 0.0) if masked else l1)
            for h in heads:
                hi, lo = _split_bf16(log_1mb[h])
                s.append(jnp.dot(hi, sums, preferred_element_type=F32)
                         + jnp.dot(lo, sums, preferred_element_type=F32))
            for h in heads:
                tail = tail_refs[h][...]
                a = jnp.exp(log_b[h] + s[h][:, :blk] + tail)
                if masked:
                    a = jnp.where(strict, a, 0.0)
                acc_refs[h][...] += jnp.dot(a.astype(BF16), v_ref[pl.ds(start, blk), lanes[h]],
                                            preferred_element_type=F32)
                tail_refs[h][...] = tail + s[h][:, blk:]
            return carry
        return step

    lax.fori_loop(0, per_tile, make_step(qt * per_tile + per_tile - 1, True), 0)
    lax.fori_loop(0, qt * per_tile, make_step(qt * per_tile - 1, False), 0)
    for h in range(SB_HEADS_PER_STEP):
        o_ref[:, h * hd:(h + 1) * hd] = acc_refs[h][...].astype(o_ref.dtype)


def stick_breaking_attention(q, kv, batch, seq):
    t, width = q.shape
    nq = seq // SB_Q_TILE
    gw = SB_HEADS_PER_STEP * SB_HEAD_DIM
    n_groups = width // gw
    return pl.pallas_call(
        functools.partial(_attn_body, scale=1.0 / math.sqrt(SB_HEAD_DIM)),
        grid=(batch, n_groups, nq),
        in_specs=[
            pl.BlockSpec((SB_Q_TILE, gw), lambda b, g, i: (b * nq + i, g)),
            pl.BlockSpec((seq, gw), lambda b, g, i: (b, g)),
            pl.BlockSpec((seq, gw), lambda b, g, i: (b, n_groups + g)),
        ],
        out_specs=pl.BlockSpec((SB_Q_TILE, gw), lambda b, g, i: (b * nq + i, g)),
        out_shape=jax.ShapeDtypeStruct((t, width), BF16),
        scratch_shapes=([pltpu.VMEM((SB_Q_TILE, SB_HEAD_DIM), F32)] * SB_HEADS_PER_STEP
                        + [pltpu.VMEM((SB_Q_TILE, SB_BLOCK), F32)] * SB_HEADS_PER_STEP),
        compiler_params=_params("arbitrary", "arbitrary", "arbitrary"),
        name="stick_breaking_attention",
    )(q, kv, kv)


def _take_best(s_ref, ids, v_ref, i_ref, e):
    s = s_ref[...]
    m = jnp.max(s, axis=0, keepdims=True)
    first = jnp.min(jnp.where(s == m, ids, jnp.inf), axis=0, keepdims=True)
    s_ref[...] = jnp.where(ids == first, -jnp.inf, s)
    v_ref[e:e + 1, :] = m
    i_ref[e:e + 1, :] = first


def _pair_candidates(v1, v2):
    k = PEER_TOPK
    tokens = v1.shape[1]
    sub = lax.broadcasted_iota(jnp.int32, (SUBLANES, tokens), 0)
    sub16 = lax.broadcasted_iota(jnp.int32, (k, tokens), 0)
    vals, ids = [], []
    vals.append(v1[0:1, :] + v2)
    ids.append(sub16)
    vals.append(jnp.where(sub16 >= 1, v1 + v2[0:1, :], -jnp.inf))
    ids.append(sub16 * k)
    for a in range(1, 5):
        b_max = k // (a + 1) - 1
        vals.append(jnp.where((sub >= 1) & (sub <= b_max), v1[a:a + 1, :] + v2[0:SUBLANES, :], -jnp.inf))
        ids.append(sub + a * k)
    vals.append(jnp.where((sub >= 5) & (sub <= 7), v1[0:SUBLANES, :] + v2[1:2, :], -jnp.inf))
    ids.append(sub * k + 1)
    return jnp.concatenate(vals, axis=0), jnp.concatenate(ids, axis=0).astype(F32)


PAIR_ROWS = 2 * PEER_TOPK + 5 * SUBLANES
ROUTE_KINDS = 4
ROUTE_PIECES = ROUTE_KINDS * PEER_HEADS


def _route_step(kind, e, head, slot, qh_ref, k1_ref, k2_ref, idx_t_ref, gate_t_ref,
                s_ref, v_ref, i_ref, cand_ref, cid_ref, sc_ref, ci_ref):
    if kind == 0:
        if e == 0:
            nt = (((1,), (1,)), ((), ()))
            q = qh_ref[head]
            s_ref[0] = lax.dot_general(k1_ref[...], q[:, :PEER_HALF], nt, preferred_element_type=F32,
                                       precision=HIGHEST)
            s_ref[1] = lax.dot_general(k2_ref[...], q[:, PEER_HALF:], nt, preferred_element_type=F32,
                                       precision=HIGHEST)
        return
    if kind in (1, 2):
        half = kind - 1
        key_ids = lax.broadcasted_iota(jnp.int32, s_ref.shape[1:], 0).astype(F32)
        _take_best(s_ref.at[half], key_ids, v_ref.at[half], i_ref.at[half], e)
        return
    if e == 0:
        cand_ref[...], cid_ref[...] = _pair_candidates(v_ref[0], v_ref[1])
    _take_best(cand_ref, cid_ref[...], sc_ref, ci_ref, e)
    if e == PEER_TOPK - 1:
        sc = sc_ref[...]
        ci = ci_ref[...].astype(jnp.int32)
        i1 = i_ref[0].astype(jnp.int32)
        i2 = i_ref[1].astype(jnp.int32)
        ca = ci >> 4
        cbi = ci & (PEER_TOPK - 1)
        e1 = jnp.zeros_like(ci)
        e2 = jnp.zeros_like(ci)
        for a in range(PEER_TOPK):
            e1 = jnp.where(ca == a, i1[a:a + 1, :], e1)
            e2 = jnp.where(cbi == a, i2[a:a + 1, :], e2)
        p = jnp.exp(sc - jnp.max(sc, axis=0, keepdims=True))
        gate = p / jnp.sum(p, axis=0, keepdims=True)
        rows = pl.ds(pl.multiple_of(head * PEER_TOPK, PEER_TOPK), PEER_TOPK)
        idx_t_ref[slot, rows, :] = e1 * PEER_N_KEYS + e2
        gate_t_ref[slot, rows, :] = gate


def _pack_pairs(x):
    half = x.shape[1] // 2
    bits = lax.bitcast_convert_type(x.astype(BF16).astype(F32), U32)
    return (bits[:, :half] >> 16) | (bits[:, half:] & jnp.uint32(0xFFFF0000))


def _unpack_pairs(w):
    lo = lax.bitcast_convert_type(w << 16, F32)
    hi = lax.bitcast_convert_type(w & jnp.uint32(0xFFFF0000), F32)
    return lo, hi


def _pack_tables_body(u_ref, v_ref, o_ref):
    half_chunks = o_ref.shape[1] // 2
    for t, tab_ref in enumerate((u_ref, v_ref)):
        packed = _pack_pairs(tab_ref[...])
        for c in range(half_chunks):
            o_ref[:, t * half_chunks + c, 0, :] = packed[:, c * LANES:(c + 1) * LANES]


def pack_tables(u_tabs, v_tabs, layer, tm=256):
    _, n, d = u_tabs.shape
    n_chunks = d // LANES
    layer_rows = pl.BlockSpec((None, tm, d), lambda i: (layer, i, 0))
    return pl.pallas_call(
        _pack_tables_body,
        grid=(n // tm,),
        in_specs=[layer_rows, layer_rows],
        out_specs=pl.BlockSpec((tm, n_chunks, 1, LANES), lambda i: (i, 0, 0, 0)),
        out_shape=jax.ShapeDtypeStruct((n, n_chunks, 1, LANES), U32),
        compiler_params=_params("arbitrary"),
        name="peer_pack_tables",
    )(u_tabs, v_tabs)


def _gelu(x):
    return 0.5 * x * (1.0 + lax.erf(x * (1.0 / math.sqrt(2.0))))


def _peer_body(qh_ref, k1_ref, k2_ref, x_ref, h_ref, tab_hbm, o_ref,
               idx_t_ref, gate_t_ref, idx_rows, gate_rows, idx_smem, buf, mix_ref, idx_sem, row_sem,
               *route_scratch, tb, n_blocks):
    step = pl.program_id(0)
    blk = step - PEER_LEAD
    mixing = blk >= 0
    n_groups = tb // SUBLANES
    route_slot = lax.rem(step, PEER_RING)

    def route_step(kind, e, head):
        _route_step(kind, e, head, route_slot, qh_ref, k1_ref, k2_ref, idx_t_ref, gate_t_ref, *route_scratch)

    def route_piece(kind, head):
        for e in range(PEER_TOPK):
            route_step(kind, e, head)

    def row_copy(j, slot, k):
        e = idx_smem[j, k]
        return pltpu.make_async_copy(tab_hbm.at[e], buf.at[slot, :, pl.ds(k, 1), :], row_sem.at[slot])

    def wait(slot):
        pltpu.make_async_copy(buf.at[slot], buf.at[slot], row_sem.at[slot]).wait()

    @pl.when(mixing)
    def _():
        slot = lax.rem(blk, PEER_RING)
        nxt = lax.rem(blk + 1, PEER_RING)
        for j in range(tb // LANES):
            cols = slice(j * LANES, (j + 1) * LANES)
            gate_rows[cols, :] = gate_t_ref[slot, :, cols].T
            idx_rows[cols, :] = idx_t_ref[slot, :, cols].astype(F32).T.astype(jnp.int32)
        head_rows = idx_t_ref[nxt, :, 0:LANES].astype(F32).T.astype(jnp.int32)
        idx_rows[tb:tb + PEER_BUFS, :] = head_rows[0:PEER_BUFS, :]
        ids = pltpu.make_async_copy(idx_rows, idx_smem, idx_sem)
        ids.start()
        ids.wait()

    @pl.when(blk == 0)
    def _():
        for j in range(PEER_AHEAD):
            def body(k, carry):
                row_copy(j, j, k).start()
                return carry
            lax.fori_loop(0, PEER_SLOTS, body, 0, unroll=8)

    d = x_ref.shape[1]
    half_chunks = d // (2 * LANES)
    row = lax.broadcasted_iota(jnp.int32, (PEER_SLOTS, LANES), 0)
    col = lax.broadcasted_iota(jnp.int32, (PEER_SLOTS, LANES), 1)
    eye = row == col
    per_chunk = PEER_SLOTS // (2 * half_chunks)

    def mix_group(g, kind):
        base = pl.multiple_of(g * SUBLANES, SUBLANES)
        rows = pl.ds(base, SUBLANES)
        gate8 = gate_rows[rows, :]
        head = g // ROUTE_KINDS
        steps_per_token = PEER_TOPK // SUBLANES

        for r in range(SUBLANES):
            ahead_slot = (r + PEER_AHEAD) % PEER_BUFS
            started = [0]

            def start_some():
                for k in range(started[0], started[0] + per_chunk):
                    row_copy(base + r + PEER_AHEAD, ahead_slot, k).start(priority=k % 2)
                started[0] += per_chunk

            wait(r)
            acc = jnp.zeros((PEER_SLOTS, LANES), F32)
            for c in range(half_chunks):
                lo, hi = _unpack_pairs(buf[r, c])
                x_lo = x_ref[rows, c * LANES:(c + 1) * LANES][r:r + 1, :]
                x_hi = x_ref[rows, (half_chunks + c) * LANES:(half_chunks + c + 1) * LANES][r:r + 1, :]
                acc = acc + lo * x_lo + hi * x_hi
                start_some()
                if c == half_chunks // 2:
                    route_step(kind, steps_per_token * r, head)
            hid = jnp.sum(acc, axis=1, keepdims=True)
            gate_row = jnp.broadcast_to(gate8[r:r + 1, :], (PEER_SLOTS, LANES))
            gate = jnp.sum(jnp.where(eye, gate_row, 0.0), axis=1, keepdims=True)
            w = jnp.broadcast_to(gate * _gelu(hid), (PEER_SLOTS, LANES))
            for c in range(half_chunks):
                lo, hi = _unpack_pairs(buf[r, half_chunks + c])
                mix_ref[r:r + 1, c * LANES:(c + 1) * LANES] = jnp.sum(lo * w, axis=0, keepdims=True)
                mix_ref[r:r + 1, (half_chunks + c) * LANES:(half_chunks + c + 1) * LANES] = (
                    jnp.sum(hi * w, axis=0, keepdims=True))
                start_some()
                if c == half_chunks // 2:
                    for e in range(steps_per_token * r + 1, steps_per_token * (r + 1)):
                        route_step(kind, e, head)
        o_ref[rows, :] = h_ref[rows, :] + mix_ref[...]

    def group(g, carry):
        kind = lax.rem(g, ROUTE_KINDS)

        @pl.when(mixing)
        def _():
            lax.switch(kind, [functools.partial(mix_group, kind=k) for k in range(ROUTE_KINDS)], g)

        @pl.when(jnp.logical_not(mixing))
        def _():
            lax.switch(kind, [functools.partial(route_piece, k) for k in range(ROUTE_KINDS)], g // ROUTE_KINDS)
        return carry

    lax.fori_loop(0, n_groups, group, 0)

    @pl.when(blk == n_blocks - 1)
    def _():
        for j in range(PEER_AHEAD):
            wait(j)


def peer_route_and_mix(xn, qh, sub_keys, h, table, tb=256):
    t, d = xn.shape
    n_blocks = t // tb
    n_chunks = d // LANES
    assert PEER_BUFS == SUBLANES and PEER_AHEAD < PEER_BUFS and PEER_SLOTS % n_chunks == 0
    assert tb % LANES == 0 and tb // SUBLANES == ROUTE_PIECES and n_blocks * tb == t
    assert table.shape[1:] == (n_chunks, 1, LANES) and qh.shape == (PEER_HEADS, t, 2 * PEER_HALF)
    mixed = lambda s: (jnp.maximum(s - PEER_LEAD, 0), 0)
    keys = pl.BlockSpec((PEER_N_KEYS, PEER_HALF), lambda s: (0, 0))
    return pl.pallas_call(
        functools.partial(_peer_body, tb=tb, n_blocks=n_blocks),
        grid=(n_blocks + PEER_LEAD,),
        in_specs=[
            pl.BlockSpec((PEER_HEADS, tb, 2 * PEER_HALF), lambda s: (0, jnp.minimum(s, n_blocks - 1), 0)),
            keys, keys,
            pl.BlockSpec((tb, d), mixed),
            pl.BlockSpec((tb, d), mixed),
            pl.BlockSpec(memory_space=pl.ANY),
        ],
        out_specs=pl.BlockSpec((tb, d), mixed),
        out_shape=jax.ShapeDtypeStruct((t, d), F32),
        scratch_shapes=[
            pltpu.VMEM((PEER_RING, PEER_SLOTS, tb), jnp.int32),
            pltpu.VMEM((PEER_RING, PEER_SLOTS, tb), F32),
            pltpu.VMEM((tb + PEER_BUFS, PEER_SLOTS), jnp.int32),
            pltpu.VMEM((tb, PEER_SLOTS), F32),
            pltpu.SMEM((tb + PEER_BUFS, PEER_SLOTS), jnp.int32),
            pltpu.VMEM((PEER_BUFS, n_chunks, PEER_SLOTS, LANES), U32),
            pltpu.VMEM((SUBLANES, d), F32),
            pltpu.SemaphoreType.DMA(()),
            pltpu.SemaphoreType.DMA((PEER_BUFS,)),
            pltpu.VMEM((2, PEER_N_KEYS, tb), F32),
            pltpu.VMEM((2, PEER_TOPK, tb), F32), pltpu.VMEM((2, PEER_TOPK, tb), F32),
            pltpu.VMEM((PAIR_ROWS, tb), F32), pltpu.VMEM((PAIR_ROWS, tb), F32),
            pltpu.VMEM((PEER_TOPK, tb), F32), pltpu.VMEM((PEER_TOPK, tb), F32),
        ],
        compiler_params=_params("arbitrary"),
        name="peer_route_mix",
    )(qh, sub_keys[0], sub_keys[1], xn, h, table)


def peer_layer(h, norm_g, w_q, sub_keys, u_tabs, v_tabs, layer):
    xn, xn_b = rmsnorm(h, [norm_g], [(0, F32), (0, BF16)])
    qh = matmul(xn_b, w_q, tn=2 * PEER_HALF, by_column_block=True, name="peer_query")
    return peer_route_and_mix(xn, qh, sub_keys, h, pack_tables(u_tabs, v_tabs, layer))


def kernel(x, norm_mix, norm_ffn, norm_final, m_in_proj, m_conv_w, m_conv_b, m_dt_bias, m_a_log, m_d, m_norm,
           m_out_proj, kv_norm, w_kv, sb_w_q, sb_w_o, peer_w_q, peer_sub_keys, peer_u, peer_v):
    batch, seq, d_model = x.shape
    t = batch * seq
    h = x.reshape(t, d_model)

    n_heads = m_dt_bias.shape[1]
    proj_w = m_in_proj.shape[2] - n_heads
    u, = rmsnorm(h, [norm_mix[0]], [(0, BF16)])
    w_in = m_in_proj[0]
    zx = matmul(u, w_in, n=proj_w, name="in_proj")
    w_dt = jnp.pad(w_in[:, proj_w:], ((0, 0), (0, LANES - n_heads)))
    dtr = matmul(u, w_dt, name="in_proj_dt")
    y = mamba_ssd(zx, dtr, m_conv_w[0], m_conv_b[0], m_dt_bias[0], m_a_log[0], m_d[0], m_norm[0], batch, seq)
    h = matmul(y, m_out_proj[0], residual=h, tn=512, name="out_proj")
    h = peer_layer(h, norm_ffn[0], peer_w_q[0], peer_sub_keys[0], peer_u, peer_v, 0)

    kvn, qn = rmsnorm(h, [kv_norm, norm_mix[1]], [(0, BF16), (1, BF16)])
    kv = matmul(kvn, w_kv, out_dtype=BF16, name="kv_proj")
    q = matmul(qn, sb_w_q[0], name="q_proj")
    o = stick_breaking_attention(q, kv, batch, seq)
    h = matmul(o, sb_w_o[0], residual=h, name="o_proj")
    h = peer_layer(h, norm_ffn[1], peer_w_q[1], peer_sub_keys[1], peer_u, peer_v, 1)

    out, = rmsnorm(h, [norm_final], [(0, F32)])
    return out.reshape(batch, seq, d_model)
```

```python
import functools
import math

import jax
import jax.numpy as jnp
from jax import lax
from jax.experimental import pallas as pl
from jax.experimental.pallas import tpu as pltpu

F32 = jnp.float32
BF16 = jnp.bfloat16
U32 = jnp.uint32
HIGHEST = lax.Precision.HIGHEST

NORM_EPS = 1e-6
LANES = 128
SUBLANES = 8

M_HEAD_DIM = 64
M_D_STATE = 128
M_GROUPS = 8
M_D_CONV = 4
M_CHUNK = 128
M_NORM_EPS = 1e-5

SB_HEAD_DIM = 128
SB_BLOCK = 128
SB_Q_TILE = 256
SB_HEADS_PER_STEP = 8

PEER_HEADS = 8
PEER_N_KEYS = 128
PEER_TOPK = 16
PEER_HALF = 128
PEER_SLOTS = PEER_HEADS * PEER_TOPK
PEER_BUFS = 8
PEER_AHEAD = 6
PEER_LEAD = 2
PEER_RING = PEER_LEAD + 1

VMEM_LIMIT = 48 * 1024 * 1024
PEER_VMEM_LIMIT = 56 * 1024 * 1024


def _params(*sem):
    return pltpu.CompilerParams(dimension_semantics=sem, vmem_limit_bytes=VMEM_LIMIT)


def _rmsnorm_body(x_ref, g_ref, *o_refs, eps, gain_of_out):
    x = x_ref[...]
    xn = x * lax.rsqrt(jnp.mean(x * x, axis=-1, keepdims=True) + eps)
    for o_ref, gi in zip(o_refs, gain_of_out):
        o_ref[...] = (xn * g_ref[gi:gi + 1, :]).astype(o_ref.dtype)


def _split_bf16(x):
    hi = x.astype(BF16)
    return hi, (x - hi.astype(F32)).astype(BF16)


def rmsnorm(x, gains, outs, tm=256):
    t, d = x.shape
    g = jnp.stack(gains)
    blk = pl.BlockSpec((tm, d), lambda i: (i, 0))
    return pl.pallas_call(
        functools.partial(_rmsnorm_body, eps=NORM_EPS, gain_of_out=tuple(gi for gi, _ in outs)),
        grid=(t // tm,),
        in_specs=[blk, pl.BlockSpec(g.shape, lambda i: (0, 0))],
        out_specs=[blk for _ in outs],
        out_shape=[jax.ShapeDtypeStruct((t, d), dt) for _, dt in outs],
        compiler_params=_params("arbitrary"),
        name="rmsnorm",
    )(x, g)


def _mm_body(a_ref, w_ref, *rest):
    o_ref, wb_ref = rest[-2:]

    @pl.when(pl.program_id(1) == 0)
    def _():
        wb_ref[...] = w_ref[...].astype(BF16)

    acc = jnp.dot(a_ref[...], wb_ref[...], preferred_element_type=F32)
    if len(rest) == 3:
        acc = rest[0][...] + acc
    o_ref[...] = acc.astype(o_ref.dtype)


def matmul(a, w, n=None, residual=None, out_dtype=F32, tm=1024, tn=1024, by_column_block=False, name="matmul"):
    m, k = a.shape
    n = w.shape[1] if n is None else n
    tn = min(tn, n)
    assert m % tm == 0 and n % tn == 0
    in_specs = [pl.BlockSpec((tm, k), lambda j, i: (i, 0)), pl.BlockSpec((k, tn), lambda j, i: (0, j))]
    args = [a, w]
    if residual is not None:
        in_specs.append(pl.BlockSpec((tm, tn), lambda j, i: (i, j)))
        args.append(residual)
    if by_column_block:
        out_spec = pl.BlockSpec((None, tm, tn), lambda j, i: (j, i, 0))
        out_shape = (n // tn, m, tn)
    else:
        out_spec = pl.BlockSpec((tm, tn), lambda j, i: (i, j))
        out_shape = (m, n)
    return pl.pallas_call(
        _mm_body,
        grid=(n // tn, m // tm),
        in_specs=in_specs,
        out_specs=out_spec,
        out_shape=jax.ShapeDtypeStruct(out_shape, out_dtype),
        scratch_shapes=[pltpu.VMEM((k, tn), BF16)],
        compiler_params=_params("arbitrary", "arbitrary"),
        name=name,
    )(*args)


def _softplus(x):
    return jnp.maximum(x, 0.0) + jnp.log1p(jnp.exp(-jnp.abs(x)))


def _silu(x):
    return x * jax.nn.sigmoid(x)


def _causal_conv_silu(raw_ref, prev_ref, w_ref, bias_ref, out_ref):
    cur = raw_ref[...]
    prev8 = prev_ref[...]
    w = w_ref[...]
    bias = bias_ref[...]
    top = cur[0:SUBLANES, :]
    row8 = lax.broadcasted_iota(jnp.int32, top.shape, 0)
    acc = cur * w[M_D_CONV - 1:M_D_CONV, :]
    acc_top = top * w[M_D_CONV - 1:M_D_CONV, :]
    for d in range(1, M_D_CONV):
        wd = w[M_D_CONV - 1 - d:M_D_CONV - d, :]
        shifted = pltpu.roll(cur, d, 0)
        acc = acc + shifted * wd
        shifted_top = jnp.where(row8 < d, pltpu.roll(prev8, d, 0), shifted[0:SUBLANES, :])
        acc_top = acc_top + shifted_top * wd
    out_ref[...] = _silu(acc + bias)
    out_ref[0:SUBLANES, :] = _silu(acc_top + bias)
    prev_ref[...] = cur[M_CHUNK - SUBLANES:M_CHUNK, :]


def _mamba_body(z_ref, x_ref, b_ref, c_ref, dtr_ref, cwx_ref, cwb_ref, cwc_ref, cbx_ref, cbb_ref, cbc_ref,
                dtb_ref, alog_ref, dexp_ref, nw_ref, e_ref, o_ref,
                state_ref, px_ref, pb_ref, pc_ref, xc_ref, bc_ref, cc_ref, y_ref):
    @pl.when(pl.program_id(1) == 0)
    def _():
        state_ref[...] = jnp.zeros_like(state_ref)
        px_ref[...] = jnp.zeros_like(px_ref)
        pb_ref[...] = jnp.zeros_like(pb_ref)
        pc_ref[...] = jnp.zeros_like(pc_ref)

    _causal_conv_silu(x_ref, px_ref, cwx_ref, cbx_ref, xc_ref)
    _causal_conv_silu(b_ref, pb_ref, cwb_ref, cbb_ref, bc_ref)
    _causal_conv_silu(c_ref, pc_ref, cwc_ref, cbc_ref, cc_ref)

    L = M_CHUNK
    row = lax.broadcasted_iota(jnp.int32, (L, L), 0)
    col = lax.broadcasted_iota(jnp.int32, (L, L), 1)
    causal = col <= row
    ltri = causal.astype(F32)

    dt = _softplus(dtr_ref[...] + dtb_ref[...])
    dta = dt * (-jnp.exp(alog_ref[...]))
    acs = jnp.dot(ltri, dta, preferred_element_type=F32, precision=HIGHEST)
    dt_t = dt.T
    acs_t = acs.T
    e = e_ref[...]

    def expand(v):
        hi, rest = v.astype(BF16), v
        rest = rest - hi.astype(F32)
        mid = rest.astype(BF16)
        lo = (rest - mid.astype(F32)).astype(BF16)
        return (jnp.dot(hi, e, preferred_element_type=F32) + jnp.dot(mid, e, preferred_element_type=F32)
                + jnp.dot(lo, e, preferred_element_type=F32))

    dt_exp = expand(dt)
    acs_exp = expand(acs)

    gw = M_HEAD_DIM * 8
    first_half = col < M_HEAD_DIM
    for g in range(M_GROUPS):
        bg = bc_ref[:, g * M_D_STATE:(g + 1) * M_D_STATE].astype(BF16)
        cg = cc_ref[:, g * M_D_STATE:(g + 1) * M_D_STATE].astype(BF16)
        xg = xc_ref[:, g * gw:(g + 1) * gw]
        ae = acs_exp[:, g * gw:(g + 1) * gw]
        de = dt_exp[:, g * gw:(g + 1) * gw]
        cb = lax.dot_general(cg, bg, (((1,), (1,)), ((), ())), preferred_element_type=F32)
        sg = state_ref[g]
        y_off = jnp.dot(cg, sg.astype(BF16), preferred_element_type=F32) * jnp.exp(ae)
        a_last = ae[L - 1:L, :]
        to_end = jnp.exp(a_last - ae) * de
        xw = (xg * to_end).astype(BF16)
        upd = lax.dot_general(bg, xw, (((0,), (0,)), ((), ())), preferred_element_type=F32)
        state_ref[g] = sg * jnp.exp(a_last) + upd
        for jp in range(4):
            xpair = xg[:, jp * LANES:(jp + 1) * LANES]
            ys = jnp.zeros((L, LANES), F32)
            for half in range(2):
                h = g * 8 + jp * 2 + half
                a_col = jnp.sum(jnp.where(col == h, acs, 0.0), axis=1, keepdims=True)
                a_row = acs_t[h:h + 1, :]
                d_row = dt_t[h:h + 1, :]
                decay = jnp.exp(jnp.where(causal, a_col - a_row, -jnp.inf))
                w = (cb * decay * d_row).astype(BF16)
                keep = first_half if half == 0 else jnp.logical_not(first_half)
                xm = jnp.where(keep, xpair, 0.0).astype(BF16)
                ys = ys + jnp.dot(w, xm, preferred_element_type=F32)
            lo = g * gw + jp * LANES
            y_ref[:, lo:lo + LANES] = (ys + y_off[:, jp * LANES:(jp + 1) * LANES]
                                       + xpair * dexp_ref[:, lo:lo + LANES])

    for g in range(M_GROUPS):
        sl = slice(g * gw, (g + 1) * gw)
        yf = y_ref[:, sl] * _silu(z_ref[:, sl])
        ms = jnp.mean(yf * yf, axis=1, keepdims=True)
        o_ref[:, sl] = (yf * lax.rsqrt(ms + M_NORM_EPS) * nw_ref[:, sl]).astype(o_ref.dtype)


def mamba_ssd(zx, dtr, conv_w, conv_b, dt_bias, a_log, d_skip, norm_w, batch, seq):
    t = zx.shape[0]
    d_inner = norm_w.shape[0]
    n_heads = d_inner // M_HEAD_DIM
    bc_w = M_GROUPS * M_D_STATE
    nc = seq // M_CHUNK
    assert d_inner % 2048 == 0 and bc_w == 1024 and zx.shape[1] == 2 * d_inner + 2 * bc_w

    pad_h = LANES - n_heads
    dtb = jnp.pad(dt_bias, (0, pad_h)).reshape(1, LANES)
    alog = jnp.pad(a_log, (0, pad_h)).reshape(1, LANES)
    dexp = jnp.repeat(d_skip, M_HEAD_DIM).reshape(1, d_inner)
    expand = (jnp.arange(LANES)[:, None] == (jnp.arange(d_inner)[None, :] // M_HEAD_DIM)).astype(BF16)
    cwx, cwb, cwc = conv_w[:, :d_inner], conv_w[:, d_inner:d_inner + bc_w], conv_w[:, d_inner + bc_w:]
    cb2 = conv_b.reshape(1, -1)
    cbx, cbb, cbc = cb2[:, :d_inner], cb2[:, d_inner:d_inner + bc_w], cb2[:, d_inner + bc_w:]

    row_map = lambda b, c: (b * nc + c, 0)
    const = lambda b, c: (0, 0)
    xblk = d_inner // bc_w
    in_specs = [
        pl.BlockSpec((M_CHUNK, d_inner), row_map),
        pl.BlockSpec((M_CHUNK, d_inner), lambda b, c: (b * nc + c, 1)),
        pl.BlockSpec((M_CHUNK, bc_w), lambda b, c: (b * nc + c, 2 * xblk)),
        pl.BlockSpec((M_CHUNK, bc_w), lambda b, c: (b * nc + c, 2 * xblk + 1)),
        pl.BlockSpec((M_CHUNK, LANES), row_map),
        pl.BlockSpec((M_D_CONV, d_inner), const), pl.BlockSpec((M_D_CONV, bc_w), const),
        pl.BlockSpec((M_D_CONV, bc_w), const),
        pl.BlockSpec((1, d_inner), const), pl.BlockSpec((1, bc_w), const), pl.BlockSpec((1, bc_w), const),
        pl.BlockSpec((1, LANES), const), pl.BlockSpec((1, LANES), const),
        pl.BlockSpec((1, d_inner), const), pl.BlockSpec((1, d_inner), const),
        pl.BlockSpec((LANES, d_inner), const),
    ]
    return pl.pallas_call(
        _mamba_body,
        grid=(batch, nc),
        in_specs=in_specs,
        out_specs=pl.BlockSpec((M_CHUNK, d_inner), row_map),
        out_shape=jax.ShapeDtypeStruct((t, d_inner), BF16),
        scratch_shapes=[
            pltpu.VMEM((M_GROUPS, M_D_STATE, M_HEAD_DIM * 8), F32),
            pltpu.VMEM((SUBLANES, d_inner), F32), pltpu.VMEM((SUBLANES, bc_w), F32),
            pltpu.VMEM((SUBLANES, bc_w), F32),
            pltpu.VMEM((M_CHUNK, d_inner), F32), pltpu.VMEM((M_CHUNK, bc_w), F32),
            pltpu.VMEM((M_CHUNK, bc_w), F32),
            pltpu.VMEM((M_CHUNK, d_inner), F32),
        ],
        compiler_params=_params("arbitrary", "arbitrary"),
        name="mamba_ssd",
    )(zx, zx, zx, zx, dtr, cwx, cwb, cwc, cbx, cbb, cbc, dtb, alog, dexp, norm_w.reshape(1, d_inner), expand)


def _attn_body(q_ref, k_ref, v_ref, o_ref, *scratch, scale):
    acc_refs = scratch[:SB_HEADS_PER_STEP]
    tail_refs = scratch[SB_HEADS_PER_STEP:]
    qt = pl.program_id(2)
    tq, blk, hd = SB_Q_TILE, SB_BLOCK, SB_HEAD_DIM
    per_tile = tq // blk
    row = lax.broadcasted_iota(jnp.int32, (tq, blk), 0)
    col = lax.broadcasted_iota(jnp.int32, (tq, blk), 1)
    r2 = lax.broadcasted_iota(jnp.int32, (blk, 2 * blk), 0)
    c2 = lax.broadcasted_iota(jnp.int32, (blk, 2 * blk), 1)
    sums = ((r2 > c2) | (c2 >= blk)).astype(BF16)
    for ref in scratch:
        ref[...] = jnp.zeros_like(ref)

    def make_step(first_kb, masked):
        def step(i, carry):
            kb = first_kb - i
            start = pl.multiple_of(kb * blk, blk)
            if masked:
                strict = (kb * blk + col) < (qt * tq + row)
            heads = range(SB_HEADS_PER_STEP)
            lanes = [slice(h * hd, (h + 1) * hd) for h in heads]
            z = [lax.dot_general((q_ref[:, lanes[h]] * scale).astype(BF16), k_ref[pl.ds(start, blk), lanes[h]],
                                 (((1,), (1,)), ((), ())), preferred_element_type=F32) for h in heads]
            log_1mb, log_b, s = [], [], []
            for h in heads:
                lp = jnp.log(1.0 + jnp.exp(-jnp.abs(z[h])))
                l1 = -(jnp.maximum(z[h], 0.0) + lp)
                log_b.append(z[h] + l1)
                log_1mb.append(jnp.where(strict, l1, 0.0) if masked else l1)
            for h in heads:
                hi, lo = _split_bf16(log_1mb[h])
                s.append(jnp.dot(hi, sums, preferred_element_type=F32)
                         + jnp.dot(lo, sums, preferred_element_type=F32))
            for h in heads:
                tail = tail_refs[h][...]
                a = jnp.exp(log_b[h] + s[h][:, :blk] + tail)
                if masked:
                    a = jnp.where(strict, a, 0.0)
                acc_refs[h][...] += jnp.dot(a.astype(BF16), v_ref[pl.ds(start, blk), lanes[h]],
                                            preferred_element_type=F32)
                tail_refs[h][...] = tail + s[h][:, blk:]
            return carry
        return step

    lax.fori_loop(0, per_tile, make_step(qt * per_tile + per_tile - 1, True), 0)
    lax.fori_loop(0, qt * per_tile, make_step(qt * per_tile - 1, False), 0)
    for h in range(SB_HEADS_PER_STEP):
        o_ref[:, h * hd:(h + 1) * hd] = acc_refs[h][...].astype(o_ref.dtype)


def stick_breaking_attention(q, kv, batch, seq):
    t, width = q.shape
    nq = seq // SB_Q_TILE
    gw = SB_HEADS_PER_STEP * SB_HEAD_DIM
    n_groups = width // gw
    return pl.pallas_call(
        functools.partial(_attn_body, scale=1.0 / math.sqrt(SB_HEAD_DIM)),
        grid=(batch, n_groups, nq),
        in_specs=[
            pl.BlockSpec((SB_Q_TILE, gw), lambda b, g, i: (b * nq + i, g)),
            pl.BlockSpec((seq, gw), lambda b, g, i: (b, g)),
            pl.BlockSpec((seq, gw), lambda b, g, i: (b, n_groups + g)),
        ],
        out_specs=pl.BlockSpec((SB_Q_TILE, gw), lambda b, g, i: (b * nq + i, g)),
        out_shape=jax.ShapeDtypeStruct((t, width), BF16),
        scratch_shapes=([pltpu.VMEM((SB_Q_TILE, SB_HEAD_DIM), F32)] * SB_HEADS_PER_STEP
                        + [pltpu.VMEM((SB_Q_TILE, SB_BLOCK), F32)] * SB_HEADS_PER_STEP),
        compiler_params=_params("arbitrary", "arbitrary", "arbitrary"),
        name="stick_breaking_attention",
    )(q, kv, kv)


def _take_best(s_ref, ids, v_ref, i_ref, e):
    s = s_ref[...]
    m = jnp.max(s, axis=0, keepdims=True)
    first = jnp.min(jnp.where(s == m, ids, jnp.inf), axis=0, keepdims=True)
    s_ref[...] = jnp.where(ids == first, -jnp.inf, s)
    v_ref[e:e + 1, :] = m
    i_ref[e:e + 1, :] = first


def _pair_candidates(v1, v2):
    k = PEER_TOPK
    tokens = v1.shape[1]
    sub = lax.broadcasted_iota(jnp.int32, (SUBLANES, tokens), 0)
    sub16 = lax.broadcasted_iota(jnp.int32, (k, tokens), 0)
    vals, ids = [], []
    vals.append(v1[0:1, :] + v2)
    ids.append(sub16)
    vals.append(jnp.where(sub16 >= 1, v1 + v2[0:1, :], -jnp.inf))
    ids.append(sub16 * k)
    for a in range(1, 5):
        b_max = k // (a + 1) - 1
        vals.append(jnp.where((sub >= 1) & (sub <= b_max), v1[a:a + 1, :] + v2[0:SUBLANES, :], -jnp.inf))
        ids.append(sub + a * k)
    vals.append(jnp.where((sub >= 5) & (sub <= 7), v1[0:SUBLANES, :] + v2[1:2, :], -jnp.inf))
    ids.append(sub * k + 1)
    return jnp.concatenate(vals, axis=0), jnp.concatenate(ids, axis=0).astype(F32)


PAIR_ROWS = 2 * PEER_TOPK + 5 * SUBLANES
ROUTE_KINDS = 4
ROUTE_PIECES = ROUTE_KINDS * PEER_HEADS


def _route_step(kind, e, head, slot, q_ref, k1_ref, k2_ref, idx_t_ref, gate_t_ref,
                s_ref, v_ref, i_ref, cand_ref, cid_ref, sc_ref, ci_ref):
    if kind == 0:
        if e == 0:
            nt = (((1,), (1,)), ((), ()))
            q = q_ref[head]
            s_ref[0] = lax.dot_general(k1_ref[...], q[:, :PEER_HALF], nt, preferred_element_type=F32,
                                       precision=HIGHEST)
            s_ref[1] = lax.dot_general(k2_ref[...], q[:, PEER_HALF:], nt, preferred_element_type=F32,
                                       precision=HIGHEST)
        return
    if kind in (1, 2):
        half = kind - 1
        key_ids = lax.broadcasted_iota(jnp.int32, s_ref.shape[1:], 0).astype(F32)
        _take_best(s_ref.at[half], key_ids, v_ref.at[half], i_ref.at[half], e)
        return
    if e == 0:
        cand_ref[...], cid_ref[...] = _pair_candidates(v_ref[0], v_ref[1])
    _take_best(cand_ref, cid_ref[...], sc_ref, ci_ref, e)
    if e == PEER_TOPK - 1:
        sc = sc_ref[...]
        ci = ci_ref[...].astype(jnp.int32)
        i1 = i_ref[0].astype(jnp.int32)
        i2 = i_ref[1].astype(jnp.int32)
        ca = ci >> 4
        cbi = ci & (PEER_TOPK - 1)
        e1 = jnp.zeros_like(ci)
        e2 = jnp.zeros_like(ci)
        for a in range(PEER_TOPK):
            e1 = jnp.where(ca == a, i1[a:a + 1, :], e1)
            e2 = jnp.where(cbi == a, i2[a:a + 1, :], e2)
        p = jnp.exp(sc - jnp.max(sc, axis=0, keepdims=True))
        gate = p / jnp.sum(p, axis=0, keepdims=True)
        rows = pl.ds(pl.multiple_of(head * PEER_TOPK, PEER_TOPK), PEER_TOPK)
        idx_t_ref[slot, rows, :] = e1 * PEER_N_KEYS + e2
        gate_t_ref[slot, rows, :] = gate


def _pack_pairs(x):
    half = x.shape[1] // 2
    bits = lax.bitcast_convert_type(x.astype(BF16).astype(F32), U32)
    return (bits[:, :half] >> 16) | (bits[:, half:] & jnp.uint32(0xFFFF0000))


def _unpack_pairs(w):
    lo = lax.bitcast_convert_type(w << 16, F32)
    hi = lax.bitcast_convert_type(w & jnp.uint32(0xFFFF0000), F32)
    return lo, hi


def _pack_tables_body(u_ref, v_ref, o_ref):
    half_chunks = o_ref.shape[1] // 2
    for t, tab_ref in enumerate((u_ref, v_ref)):
        packed = _pack_pairs(tab_ref[...])
        for c in range(half_chunks):
            o_ref[:, t * half_chunks + c, 0, :] = packed[:, c * LANES:(c + 1) * LANES]


def pack_tables(u_tabs, v_tabs, layer, tm=256):
    _, n, d = u_tabs.shape
    n_chunks = d // LANES
    layer_rows = pl.BlockSpec((None, tm, d), lambda i: (layer, i, 0))
    return pl.pallas_call(
        _pack_tables_body,
        grid=(n // tm,),
        in_specs=[layer_rows, layer_rows],
        out_specs=pl.BlockSpec((tm, n_chunks, 1, LANES), lambda i: (i, 0, 0, 0)),
        out_shape=jax.ShapeDtypeStruct((n, n_chunks, 1, LANES), U32),
        compiler_params=_params("arbitrary"),
        name="peer_pack_tables",
    )(u_tabs, v_tabs)


def _gelu(x):
    return 0.5 * x * (1.0 + lax.erf(x * (1.0 / math.sqrt(2.0))))


def _peer_body(xb0_ref, xbn_ref, wq_ref, k1_ref, k2_ref, x_ref, h_ref, pg_ref, tab_hbm, o_ref, *rest,
               tb, n_blocks, post_gain_of_out):
    n_post = len(post_gain_of_out)
    post_refs = rest[:n_post]
    (q_ring, idx_t_ref, gate_t_ref, idx_rows, gate_rows, idx_smem, buf, mix_ref, idx_sem,
     row_sem) = rest[n_post:n_post + 10]
    route_scratch = rest[n_post + 10:]
    step = pl.program_id(0)
    blk = step - PEER_LEAD
    mixing = blk >= 0
    n_groups = tb // SUBLANES
    route_slot = lax.rem(step, PEER_RING)
    q_cur = q_ring.at[lax.rem(step, 2)]
    q_next = q_ring.at[lax.rem(step + 1, 2)]
    kc_w = wq_ref.shape[1] // (2 * ROUTE_KINDS)

    @pl.when(step == 0)
    def _():
        for hd in range(PEER_HEADS):
            q_cur[hd] = jnp.dot(xb0_ref[...], wq_ref[hd], preferred_element_type=F32)

    def project(kind, head, u):
        kc = 2 * kind + u
        ks = slice(kc * kc_w, (kc + 1) * kc_w)
        part = jnp.dot(xbn_ref[:, ks], wq_ref[head, ks, :], preferred_element_type=F32)
        if kc == 0:
            q_next[head] = part
        else:
            q_next[head] += part

    def route_step(kind, e, head):
        _route_step(kind, e, head, route_slot, q_cur, k1_ref, k2_ref, idx_t_ref, gate_t_ref, *route_scratch)

    def route_piece(kind, head):
        for u in range(2):
            project(kind, head, u)
        for e in range(PEER_TOPK):
            route_step(kind, e, head)

    def row_copy(j, slot, k):
        e = idx_smem[j, k]
        return pltpu.make_async_copy(tab_hbm.at[e], buf.at[slot, :, pl.ds(k, 1), :], row_sem.at[slot])

    def wait(slot):
        pltpu.make_async_copy(buf.at[slot], buf.at[slot], row_sem.at[slot]).wait()

    @pl.when(mixing)
    def _():
        slot = lax.rem(blk, PEER_RING)
        nxt = lax.rem(blk + 1, PEER_RING)
        for j in range(tb // LANES):
            cols = slice(j * LANES, (j + 1) * LANES)
            gate_rows[cols, :] = gate_t_ref[slot, :, cols].T
            idx_rows[cols, :] = idx_t_ref[slot, :, cols].astype(F32).T.astype(jnp.int32)
        head_rows = idx_t_ref[nxt, :, 0:LANES].astype(F32).T.astype(jnp.int32)
        idx_rows[tb:tb + PEER_BUFS, :] = head_rows[0:PEER_BUFS, :]
        ids = pltpu.make_async_copy(idx_rows, idx_smem, idx_sem)
        ids.start()
        ids.wait()

    @pl.when(blk == 0)
    def _():
        for j in range(PEER_AHEAD):
            def body(k, carry):
                row_copy(j, j, k).start()
                return carry
            lax.fori_loop(0, PEER_SLOTS, body, 0, unroll=8)

    d = x_ref.shape[1]
    half_chunks = d // (2 * LANES)
    row = lax.broadcasted_iota(jnp.int32, (PEER_SLOTS, LANES), 0)
    col = lax.broadcasted_iota(jnp.int32, (PEER_SLOTS, LANES), 1)
    eye = row == col
    per_chunk = PEER_SLOTS // (2 * half_chunks)

    def mix_group(g, kind):
        base = pl.multiple_of(g * SUBLANES, SUBLANES)
        rows = pl.ds(base, SUBLANES)
        gate8 = gate_rows[rows, :]
        head = g // ROUTE_KINDS
        steps_per_token = PEER_TOPK // SUBLANES

        for r in range(SUBLANES):
            ahead_slot = (r + PEER_AHEAD) % PEER_BUFS
            started = [0]

            def start_some():
                for k in range(started[0], started[0] + per_chunk):
                    row_copy(base + r + PEER_AHEAD, ahead_slot, k).start(priority=k % 2)
                started[0] += per_chunk

            wait(r)
            acc = jnp.zeros((PEER_SLOTS, LANES), F32)
            for c in range(half_chunks):
                lo, hi = _unpack_pairs(buf[r, c])
                x_lo = x_ref[rows, c * LANES:(c + 1) * LANES][r:r + 1, :]
                x_hi = x_ref[rows, (half_chunks + c) * LANES:(half_chunks + c + 1) * LANES][r:r + 1, :]
                acc = acc + lo * x_lo + hi * x_hi
                start_some()
                if c == 0 and r in (2, 5):
                    project(kind, head, r // 3)
                if c == half_chunks // 2:
                    route_step(kind, steps_per_token * r, head)
            hid = jnp.sum(acc, axis=1, keepdims=True)
            gate_row = jnp.broadcast_to(gate8[r:r + 1, :], (PEER_SLOTS, LANES))
            gate = jnp.sum(jnp.where(eye, gate_row, 0.0), axis=1, keepdims=True)
            w = jnp.broadcast_to(gate * _gelu(hid), (PEER_SLOTS, LANES))
            for c in range(half_chunks):
                lo, hi = _unpack_pairs(buf[r, half_chunks + c])
                mix_ref[r:r + 1, c * LANES:(c + 1) * LANES] = jnp.sum(lo * w, axis=0, keepdims=True)
                mix_ref[r:r + 1, (half_chunks + c) * LANES:(half_chunks + c + 1) * LANES] = (
                    jnp.sum(hi * w, axis=0, keepdims=True))
                start_some()
                if c == half_chunks // 2:
                    for e in range(steps_per_token * r + 1, steps_per_token * (r + 1)):
                        route_step(kind, e, head)
        y = h_ref[rows, :] + mix_ref[...]
        o_ref[rows, :] = y
        if post_refs:
            yn = y * lax.rsqrt(jnp.mean(y * y, axis=-1, keepdims=True) + NORM_EPS)
            for p_ref, gi in zip(post_refs, post_gain_of_out):
                p_ref[rows, :] = (yn * pg_ref[gi:gi + 1, :]).astype(p_ref.dtype)

    def group(g, carry):
        kind = lax.rem(g, ROUTE_KINDS)

        @pl.when(mixing)
        def _():
            lax.switch(kind, [functools.partial(mix_group, kind=k) for k in range(ROUTE_KINDS)], g)

        @pl.when(jnp.logical_not(mixing))
        def _():
            lax.switch(kind, [functools.partial(route_piece, k) for k in range(ROUTE_KINDS)], g // ROUTE_KINDS)
        return carry

    lax.fori_loop(0, n_groups, group, 0)

    @pl.when(blk == n_blocks - 1)
    def _():
        for j in range(PEER_AHEAD):
            wait(j)


def peer_route_and_mix(xn, xn_b, w_q, sub_keys, h, table, post_gains, post_outs, tb=256):
    t, d = xn.shape
    n_blocks = t // tb
    n_chunks = d // LANES
    qw = 2 * PEER_HALF
    assert PEER_BUFS == SUBLANES and PEER_AHEAD < PEER_BUFS and PEER_SLOTS % n_chunks == 0
    assert tb % LANES == 0 and tb // SUBLANES == ROUTE_PIECES and n_blocks * tb == t
    assert table.shape[1:] == (n_chunks, 1, LANES) and w_q.shape == (d, PEER_HEADS * qw)
    wq_heads = w_q.astype(BF16).reshape(d, PEER_HEADS, qw).transpose(1, 0, 2)
    pg = jnp.stack(post_gains)
    mixed = lambda s: (jnp.maximum(s - PEER_LEAD, 0), 0)
    keys = pl.BlockSpec((PEER_N_KEYS, PEER_HALF), lambda s: (0, 0))
    row_blk = pl.BlockSpec((tb, d), mixed)
    return pl.pallas_call(
        functools.partial(_peer_body, tb=tb, n_blocks=n_blocks, post_gain_of_out=tuple(gi for gi, _ in post_outs)),
        grid=(n_blocks + PEER_LEAD,),
        in_specs=[
            pl.BlockSpec((tb, d), lambda s: (0, 0)),
            pl.BlockSpec((tb, d), lambda s: (jnp.minimum(s + 1, n_blocks - 1), 0)),
            pl.BlockSpec((PEER_HEADS, d, qw), lambda s: (0, 0, 0)),
            keys, keys,
            row_blk,
            row_blk,
            pl.BlockSpec(pg.shape, lambda s: (0, 0)),
            pl.BlockSpec(memory_space=pl.ANY),
        ],
        out_specs=[row_blk] + [row_blk for _ in post_outs],
        out_shape=[jax.ShapeDtypeStruct((t, d), F32)] + [jax.ShapeDtypeStruct((t, d), dt) for _, dt in post_outs],
        scratch_shapes=[
            pltpu.VMEM((2, PEER_HEADS, tb, qw), F32),
            pltpu.VMEM((PEER_RING, PEER_SLOTS, tb), jnp.int32),
            pltpu.VMEM((PEER_RING, PEER_SLOTS, tb), F32),
            pltpu.VMEM((tb + PEER_BUFS, PEER_SLOTS), jnp.int32),
            pltpu.VMEM((tb, PEER_SLOTS), F32),
            pltpu.SMEM((tb + PEER_BUFS, PEER_SLOTS), jnp.int32),
            pltpu.VMEM((PEER_BUFS, n_chunks, PEER_SLOTS, LANES), U32),
            pltpu.VMEM((SUBLANES, d), F32),
            pltpu.SemaphoreType.DMA(()),
            pltpu.SemaphoreType.DMA((PEER_BUFS,)),
            pltpu.VMEM((2, PEER_N_KEYS, tb), F32),
            pltpu.VMEM((2, PEER_TOPK, tb), F32), pltpu.VMEM((2, PEER_TOPK, tb), F32),
            pltpu.VMEM((PAIR_ROWS, tb), F32), pltpu.VMEM((PAIR_ROWS, tb), F32),
            pltpu.VMEM((PEER_TOPK, tb), F32), pltpu.VMEM((PEER_TOPK, tb), F32),
        ],
        compiler_params=pltpu.CompilerParams(dimension_semantics=("arbitrary",), vmem_limit_bytes=PEER_VMEM_LIMIT),
        name="peer_route_mix",
    )(xn_b, xn_b, wq_heads, sub_keys[0], sub_keys[1], xn, h, pg, table)


def peer_layer(h, norm_g, w_q, sub_keys, u_tabs, v_tabs, layer, post_gains, post_outs):
    xn, xn_b = rmsnorm(h, [norm_g], [(0, F32), (0, BF16)])
    return peer_route_and_mix(xn, xn_b, w_q, sub_keys, h, pack_tables(u_tabs, v_tabs, layer), post_gains, post_outs)


def kernel(x, norm_mix, norm_ffn, norm_final, m_in_proj, m_conv_w, m_conv_b, m_dt_bias, m_a_log, m_d, m_norm,
           m_out_proj, kv_norm, w_kv, sb_w_q, sb_w_o, peer_w_q, peer_sub_keys, peer_u, peer_v):
    batch, seq, d_model = x.shape
    t = batch * seq
    h = x.reshape(t, d_model)

    n_heads = m_dt_bias.shape[1]
    proj_w = m_in_proj.shape[2] - n_heads
    u, = rmsnorm(h, [norm_mix[0]], [(0, BF16)])
    w_in = m_in_proj[0]
    zx = matmul(u, w_in, n=proj_w, name="in_proj")
    w_dt = jnp.pad(w_in[:, proj_w:], ((0, 0), (0, LANES - n_heads)))
    dtr = matmul(u, w_dt, name="in_proj_dt")
    y = mamba_ssd(zx, dtr, m_conv_w[0], m_conv_b[0], m_dt_bias[0], m_a_log[0], m_d[0], m_norm[0], batch, seq)
    h = matmul(y, m_out_proj[0], residual=h, tn=512, name="out_proj")
    h, kvn, qn = peer_layer(h, norm_ffn[0], peer_w_q[0], peer_sub_keys[0], peer_u, peer_v, 0,
                            [kv_norm, norm_mix[1]], [(0, BF16), (1, BF16)])

    kv = matmul(kvn, w_kv, out_dtype=BF16, name="kv_proj")
    q = matmul(qn, sb_w_q[0], name="q_proj")
    o = stick_breaking_attention(q, kv, batch, seq)
    h = matmul(o, sb_w_o[0], residual=h, name="o_proj")
    _, out = peer_layer(h, norm_ffn[1], peer_w_q[1], peer_sub_keys[1], peer_u, peer_v, 1,
                        [norm_final], [(0, F32)])
    return out.reshape(batch, seq, d_model)
```

```python
import functools
import math

import jax
import jax.numpy as jnp
from jax import lax
from jax.experimental import pallas as pl
from jax.experimental.pallas import tpu as pltpu

F32 = jnp.float32
BF16 = jnp.bfloat16
U32 = jnp.uint32
HIGHEST = lax.Precision.HIGHEST

NORM_EPS = 1e-6
LANES = 128
SUBLANES = 8

M_HEAD_DIM = 64
M_D_STATE = 128
M_GROUPS = 8
M_D_CONV = 4
M_CHUNK = 128
M_NORM_EPS = 1e-5

SB_HEAD_DIM = 128
SB_BLOCK = 128
SB_Q_TILE = 256
SB_HEADS_PER_STEP = 8

PEER_HEADS = 8
PEER_N_KEYS = 128
PEER_TOPK = 16
PEER_HALF = 128
PEER_SLOTS = PEER_HEADS * PEER_TOPK
PEER_BUFS = 8
PEER_AHEAD = 6
PEER_LEAD = 2
PEER_RING = PEER_LEAD + 1

VMEM_LIMIT = 48 * 1024 * 1024
PEER_VMEM_LIMIT = 56 * 1024 * 1024


def _params(*sem):
    return pltpu.CompilerParams(dimension_semantics=sem, vmem_limit_bytes=VMEM_LIMIT)


def _rmsnorm_body(x_ref, g_ref, *o_refs, eps, gain_of_out):
    x = x_ref[...]
    xn = x * lax.rsqrt(jnp.mean(x * x, axis=-1, keepdims=True) + eps)
    for o_ref, gi in zip(o_refs, gain_of_out):
        o_ref[...] = (xn * g_ref[gi:gi + 1, :]).astype(o_ref.dtype)


def _split_bf16(x):
    hi = x.astype(BF16)
    return hi, (x - hi.astype(F32)).astype(BF16)


def rmsnorm(x, gains, outs, tm=256):
    t, d = x.shape
    g = jnp.stack(gains)
    blk = pl.BlockSpec((tm, d), lambda i: (i, 0))
    return pl.pallas_call(
        functools.partial(_rmsnorm_body, eps=NORM_EPS, gain_of_out=tuple(gi for gi, _ in outs)),
        grid=(t // tm,),
        in_specs=[blk, pl.BlockSpec(g.shape, lambda i: (0, 0))],
        out_specs=[blk for _ in outs],
        out_shape=[jax.ShapeDtypeStruct((t, d), dt) for _, dt in outs],
        compiler_params=_params("arbitrary"),
        name="rmsnorm",
    )(x, g)


def _mm_body(a_ref, w_ref, *rest):
    o_ref, wb_ref = rest[-2:]

    @pl.when(pl.program_id(1) == 0)
    def _():
        wb_ref[...] = w_ref[...].astype(BF16)

    acc = jnp.dot(a_ref[...], wb_ref[...], preferred_element_type=F32)
    if len(rest) == 3:
        acc = rest[0][...] + acc
    o_ref[...] = acc.astype(o_ref.dtype)


def matmul(a, w, n=None, residual=None, out_dtype=F32, tm=1024, tn=1024, name="matmul"):
    m, k = a.shape
    n = w.shape[1] if n is None else n
    tn = min(tn, n)
    assert m % tm == 0 and n % tn == 0
    in_specs = [pl.BlockSpec((tm, k), lambda j, i: (i, 0)), pl.BlockSpec((k, tn), lambda j, i: (0, j))]
    args = [a, w]
    if residual is not None:
        in_specs.append(pl.BlockSpec((tm, tn), lambda j, i: (i, j)))
        args.append(residual)
    return pl.pallas_call(
        _mm_body,
        grid=(n // tn, m // tm),
        in_specs=in_specs,
        out_specs=pl.BlockSpec((tm, tn), lambda j, i: (i, j)),
        out_shape=jax.ShapeDtypeStruct((m, n), out_dtype),
        scratch_shapes=[pltpu.VMEM((k, tn), BF16)],
        compiler_params=_params("arbitrary", "arbitrary"),
        name=name,
    )(*args)


def _softplus(x):
    return jnp.maximum(x, 0.0) + jnp.log1p(jnp.exp(-jnp.abs(x)))


def _silu(x):
    return x * jax.nn.sigmoid(x)


def _causal_conv_silu(raw_ref, prev_ref, w_ref, bias_ref, out_ref):
    cur = raw_ref[...]
    prev8 = prev_ref[...]
    w = w_ref[...]
    bias = bias_ref[...]
    top = cur[0:SUBLANES, :]
    row8 = lax.broadcasted_iota(jnp.int32, top.shape, 0)
    acc = cur * w[M_D_CONV - 1:M_D_CONV, :]
    acc_top = top * w[M_D_CONV - 1:M_D_CONV, :]
    for d in range(1, M_D_CONV):
        wd = w[M_D_CONV - 1 - d:M_D_CONV - d, :]
        shifted = pltpu.roll(cur, d, 0)
        acc = acc + shifted * wd
        shifted_top = jnp.where(row8 < d, pltpu.roll(prev8, d, 0), shifted[0:SUBLANES, :])
        acc_top = acc_top + shifted_top * wd
    out_ref[...] = _silu(acc + bias)
    out_ref[0:SUBLANES, :] = _silu(acc_top + bias)
    prev_ref[...] = cur[M_CHUNK - SUBLANES:M_CHUNK, :]


def _mamba_body(z_ref, x_ref, b_ref, c_ref, dtr_ref, cwx_ref, cwb_ref, cwc_ref, cbx_ref, cbb_ref, cbc_ref,
                dtb_ref, alog_ref, dexp_ref, nw_ref, e_ref, o_ref,
                state_ref, px_ref, pb_ref, pc_ref, xc_ref, bc_ref, cc_ref, y_ref):
    @pl.when(pl.program_id(1) == 0)
    def _():
        state_ref[...] = jnp.zeros_like(state_ref)
        px_ref[...] = jnp.zeros_like(px_ref)
        pb_ref[...] = jnp.zeros_like(pb_ref)
        pc_ref[...] = jnp.zeros_like(pc_ref)

    _causal_conv_silu(x_ref, px_ref, cwx_ref, cbx_ref, xc_ref)
    _causal_conv_silu(b_ref, pb_ref, cwb_ref, cbb_ref, bc_ref)
    _causal_conv_silu(c_ref, pc_ref, cwc_ref, cbc_ref, cc_ref)

    L = M_CHUNK
    row = lax.broadcasted_iota(jnp.int32, (L, L), 0)
    col = lax.broadcasted_iota(jnp.int32, (L, L), 1)
    causal = col <= row
    ltri = causal.astype(F32)

    dt = _softplus(dtr_ref[...] + dtb_ref[...])
    dta = dt * (-jnp.exp(alog_ref[...]))
    acs = jnp.dot(ltri, dta, preferred_element_type=F32, precision=HIGHEST)
    dt_t = dt.T
    acs_t = acs.T
    e = e_ref[...]

    def expand(v):
        hi, rest = v.astype(BF16), v
        rest = rest - hi.astype(F32)
        mid = rest.astype(BF16)
        lo = (rest - mid.astype(F32)).astype(BF16)
        return (jnp.dot(hi, e, preferred_element_type=F32) + jnp.dot(mid, e, preferred_element_type=F32)
                + jnp.dot(lo, e, preferred_element_type=F32))

    dt_exp = expand(dt)
    acs_exp = expand(acs)

    gw = M_HEAD_DIM * 8
    first_half = col < M_HEAD_DIM
    for g in range(M_GROUPS):
        bg = bc_ref[:, g * M_D_STATE:(g + 1) * M_D_STATE].astype(BF16)
        cg = cc_ref[:, g * M_D_STATE:(g + 1) * M_D_STATE].astype(BF16)
        xg = xc_ref[:, g * gw:(g + 1) * gw]
        ae = acs_exp[:, g * gw:(g + 1) * gw]
        de = dt_exp[:, g * gw:(g + 1) * gw]
        cb = lax.dot_general(cg, bg, (((1,), (1,)), ((), ())), preferred_element_type=F32)
        sg = state_ref[g]
        y_off = jnp.dot(cg, sg.astype(BF16), preferred_element_type=F32) * jnp.exp(ae)
        a_last = ae[L - 1:L, :]
        to_end = jnp.exp(a_last - ae) * de
        xw = (xg * to_end).astype(BF16)
        upd = lax.dot_general(bg, xw, (((0,), (0,)), ((), ())), preferred_element_type=F32)
        state_ref[g] = sg * jnp.exp(a_last) + upd
        for jp in range(4):
            xpair = xg[:, jp * LANES:(jp + 1) * LANES]
            ys = jnp.zeros((L, LANES), F32)
            for half in range(2):
                h = g * 8 + jp * 2 + half
                a_col = jnp.sum(jnp.where(col == h, acs, 0.0), axis=1, keepdims=True)
                a_row = acs_t[h:h + 1, :]
                d_row = dt_t[h:h + 1, :]
                decay = jnp.exp(jnp.where(causal, a_col - a_row, -jnp.inf))
                w = (cb * decay * d_row).astype(BF16)
                keep = first_half if half == 0 else jnp.logical_not(first_half)
                xm = jnp.where(keep, xpair, 0.0).astype(BF16)
                ys = ys + jnp.dot(w, xm, preferred_element_type=F32)
            lo = g * gw + jp * LANES
            y_ref[:, lo:lo + LANES] = (ys + y_off[:, jp * LANES:(jp + 1) * LANES]
                                       + xpair * dexp_ref[:, lo:lo + LANES])

    for g in range(M_GROUPS):
        sl = slice(g * gw, (g + 1) * gw)
        yf = y_ref[:, sl] * _silu(z_ref[:, sl])
        ms = jnp.mean(yf * yf, axis=1, keepdims=True)
        o_ref[:, sl] = (yf * lax.rsqrt(ms + M_NORM_EPS) * nw_ref[:, sl]).astype(o_ref.dtype)


def mamba_ssd(zx, dtr, conv_w, conv_b, dt_bias, a_log, d_skip, norm_w, batch, seq):
    t = zx.shape[0]
    d_inner = norm_w.shape[0]
    n_heads = d_inner // M_HEAD_DIM
    bc_w = M_GROUPS * M_D_STATE
    nc = seq // M_CHUNK
    assert d_inner % 2048 == 0 and bc_w == 1024 and zx.shape[1] == 2 * d_inner + 2 * bc_w

    pad_h = LANES - n_heads
    dtb = jnp.pad(dt_bias, (0, pad_h)).reshape(1, LANES)
    alog = jnp.pad(a_log, (0, pad_h)).reshape(1, LANES)
    dexp = jnp.repeat(d_skip, M_HEAD_DIM).reshape(1, d_inner)
    expand = (jnp.arange(LANES)[:, None] == (jnp.arange(d_inner)[None, :] // M_HEAD_DIM)).astype(BF16)
    cwx, cwb, cwc = conv_w[:, :d_inner], conv_w[:, d_inner:d_inner + bc_w], conv_w[:, d_inner + bc_w:]
    cb2 = conv_b.reshape(1, -1)
    cbx, cbb, cbc = cb2[:, :d_inner], cb2[:, d_inner:d_inner + bc_w], cb2[:, d_inner + bc_w:]

    row_map = lambda b, c: (b * nc + c, 0)
    const = lambda b, c: (0, 0)
    xblk = d_inner // bc_w
    in_specs = [
        pl.BlockSpec((M_CHUNK, d_inner), row_map),
        pl.BlockSpec((M_CHUNK, d_inner), lambda b, c: (b * nc + c, 1)),
        pl.BlockSpec((M_CHUNK, bc_w), lambda b, c: (b * nc + c, 2 * xblk)),
        pl.BlockSpec((M_CHUNK, bc_w), lambda b, c: (b * nc + c, 2 * xblk + 1)),
        pl.BlockSpec((M_CHUNK, LANES), row_map),
        pl.BlockSpec((M_D_CONV, d_inner), const), pl.BlockSpec((M_D_CONV, bc_w), const),
        pl.BlockSpec((M_D_CONV, bc_w), const),
        pl.BlockSpec((1, d_inner), const), pl.BlockSpec((1, bc_w), const), pl.BlockSpec((1, bc_w), const),
        pl.BlockSpec((1, LANES), const), pl.BlockSpec((1, LANES), const),
        pl.BlockSpec((1, d_inner), const), pl.BlockSpec((1, d_inner), const),
        pl.BlockSpec((LANES, d_inner), const),
    ]
    return pl.pallas_call(
        _mamba_body,
        grid=(batch, nc),
        in_specs=in_specs,
        out_specs=pl.BlockSpec((M_CHUNK, d_inner), row_map),
        out_shape=jax.ShapeDtypeStruct((t, d_inner), BF16),
        scratch_shapes=[
            pltpu.VMEM((M_GROUPS, M_D_STATE, M_HEAD_DIM * 8), F32),
            pltpu.VMEM((SUBLANES, d_inner), F32), pltpu.VMEM((SUBLANES, bc_w), F32),
            pltpu.VMEM((SUBLANES, bc_w), F32),
            pltpu.VMEM((M_CHUNK, d_inner), F32), pltpu.VMEM((M_CHUNK, bc_w), F32),
            pltpu.VMEM((M_CHUNK, bc_w), F32),
            pltpu.VMEM((M_CHUNK, d_inner), F32),
        ],
        compiler_params=_params("arbitrary", "arbitrary"),
        name="mamba_ssd",
    )(zx, zx, zx, zx, dtr, cwx, cwb, cwc, cbx, cbb, cbc, dtb, alog, dexp, norm_w.reshape(1, d_inner), expand)


def _attn_body(q_ref, k_ref, v_ref, o_ref, *scratch, scale):
    acc_refs = scratch[:SB_HEADS_PER_STEP]
    tail_refs = scratch[SB_HEADS_PER_STEP:]
    qt = pl.program_id(2)
    tq, blk, hd = SB_Q_TILE, SB_BLOCK, SB_HEAD_DIM
    per_tile = tq // blk
    row = lax.broadcasted_iota(jnp.int32, (tq, blk), 0)
    col = lax.broadcasted_iota(jnp.int32, (tq, blk), 1)
    r2 = lax.broadcasted_iota(jnp.int32, (blk, 2 * blk), 0)
    c2 = lax.broadcasted_iota(jnp.int32, (blk, 2 * blk), 1)
    sums = ((r2 > c2) | (c2 >= blk)).astype(BF16)
    for ref in scratch:
        ref[...] = jnp.zeros_like(ref)

    def make_step(first_kb, masked):
        def step(i, carry):
            kb = first_kb - i
            start = pl.multiple_of(kb * blk, blk)
            if masked:
                strict = (kb * blk + col) < (qt * tq + row)
            heads = range(SB_HEADS_PER_STEP)
            lanes = [slice(h * hd, (h + 1) * hd) for h in heads]
            z = [lax.dot_general((q_ref[:, lanes[h]] * scale).astype(BF16), k_ref[pl.ds(start, blk), lanes[h]],
                                 (((1,), (1,)), ((), ())), preferred_element_type=F32) for h in heads]
            log_1mb, log_b, s = [], [], []
            for h in heads:
                lp = jnp.log(1.0 + jnp.exp(-jnp.abs(z[h])))
                l1 = -(jnp.maximum(z[h], 0.0) + lp)
                log_b.append(z[h] + l1)
                log_1mb.append(jnp.where(strict, l1, 0.0) if masked else l1)
            for h in heads:
                hi, lo = _split_bf16(log_1mb[h])
                s.append(jnp.dot(hi, sums, preferred_element_type=F32)
                         + jnp.dot(lo, sums, preferred_element_type=F32))
            for h in heads:
                tail = tail_refs[h][...]
                a = jnp.exp(log_b[h] + s[h][:, :blk] + tail)
                if masked:
                    a = jnp.where(strict, a, 0.0)
                acc_refs[h][...] += jnp.dot(a.astype(BF16), v_ref[pl.ds(start, blk), lanes[h]],
                                            preferred_element_type=F32)
                tail_refs[h][...] = tail + s[h][:, blk:]
            return carry
        return step

    lax.fori_loop(0, per_tile, make_step(qt * per_tile + per_tile - 1, True), 0)
    lax.fori_loop(0, qt * per_tile, make_step(qt * per_tile - 1, False), 0)
    for h in range(SB_HEADS_PER_STEP):
        o_ref[:, h * hd:(h + 1) * hd] = acc_refs[h][...].astype(o_ref.dtype)


def stick_breaking_attention(q, kv, batch, seq):
    t, width = q.shape
    nq = seq // SB_Q_TILE
    gw = SB_HEADS_PER_STEP * SB_HEAD_DIM
    n_groups = width // gw
    return pl.pallas_call(
        functools.partial(_attn_body, scale=1.0 / math.sqrt(SB_HEAD_DIM)),
        grid=(batch, n_groups, nq),
        in_specs=[
            pl.BlockSpec((SB_Q_TILE, gw), lambda b, g, i: (b * nq + i, g)),
            pl.BlockSpec((seq, gw), lambda b, g, i: (b, g)),
            pl.BlockSpec((seq, gw), lambda b, g, i: (b, n_groups + g)),
        ],
        out_specs=pl.BlockSpec((SB_Q_TILE, gw), lambda b, g, i: (b * nq + i, g)),
        out_shape=jax.ShapeDtypeStruct((t, width), BF16),
        scratch_shapes=([pltpu.VMEM((SB_Q_TILE, SB_HEAD_DIM), F32)] * SB_HEADS_PER_STEP
                        + [pltpu.VMEM((SB_Q_TILE, SB_BLOCK), F32)] * SB_HEADS_PER_STEP),
        compiler_params=_params("arbitrary", "arbitrary", "arbitrary"),
        name="stick_breaking_attention",
    )(q, kv, kv)


def _take_best(s_ref, ids, v_ref, i_ref, e):
    s = s_ref[...]
    m = jnp.max(s, axis=0, keepdims=True)
    first = jnp.min(jnp.where(s == m, ids, jnp.inf), axis=0, keepdims=True)
    s_ref[...] = jnp.where(ids == first, -jnp.inf, s)
    v_ref[e:e + 1, :] = m
    i_ref[e:e + 1, :] = first


def _pair_candidates(v1, v2):
    k = PEER_TOPK
    tokens = v1.shape[1]
    sub = lax.broadcasted_iota(jnp.int32, (SUBLANES, tokens), 0)
    sub16 = lax.broadcasted_iota(jnp.int32, (k, tokens), 0)
    vals, ids = [], []
    vals.append(v1[0:1, :] + v2)
    ids.append(sub16)
    vals.append(jnp.where(sub16 >= 1, v1 + v2[0:1, :], -jnp.inf))
    ids.append(sub16 * k)
    for a in range(1, 5):
        b_max = k // (a + 1) - 1
        vals.append(jnp.where((sub >= 1) & (sub <= b_max), v1[a:a + 1, :] + v2[0:SUBLANES, :], -jnp.inf))
        ids.append(sub + a * k)
    vals.append(jnp.where((sub >= 5) & (sub <= 7), v1[0:SUBLANES, :] + v2[1:2, :], -jnp.inf))
    ids.append(sub * k + 1)
    return jnp.concatenate(vals, axis=0), jnp.concatenate(ids, axis=0).astype(F32)


PAIR_ROWS = 2 * PEER_TOPK + 5 * SUBLANES
ROUTE_KINDS = 4
ROUTE_PIECES = ROUTE_KINDS * PEER_HEADS


def _route_step(kind, e, head, slot, q_ref, k1_ref, k2_ref, idx_t_ref, gate_t_ref,
                s_ref, v_ref, i_ref, cand_ref, cid_ref, sc_ref, ci_ref):
    if kind == 0:
        if e == 0:
            nt = (((1,), (1,)), ((), ()))
            q = q_ref[head]
            s_ref[0] = lax.dot_general(k1_ref[...], q[:, :PEER_HALF], nt, preferred_element_type=F32,
                                       precision=HIGHEST)
            s_ref[1] = lax.dot_general(k2_ref[...], q[:, PEER_HALF:], nt, preferred_element_type=F32,
                                       precision=HIGHEST)
        return
    if kind in (1, 2):
        half = kind - 1
        key_ids = lax.broadcasted_iota(jnp.int32, s_ref.shape[1:], 0).astype(F32)
        _take_best(s_ref.at[half], key_ids, v_ref.at[half], i_ref.at[half], e)
        return
    if e == 0:
        cand_ref[...], cid_ref[...] = _pair_candidates(v_ref[0], v_ref[1])
    _take_best(cand_ref, cid_ref[...], sc_ref, ci_ref, e)
    if e == PEER_TOPK - 1:
        sc = sc_ref[...]
        ci = ci_ref[...].astype(jnp.int32)
        i1 = i_ref[0].astype(jnp.int32)
        i2 = i_ref[1].astype(jnp.int32)
        ca = ci >> 4
        cbi = ci & (PEER_TOPK - 1)
        e1 = jnp.zeros_like(ci)
        e2 = jnp.zeros_like(ci)
        for a in range(PEER_TOPK):
            e1 = jnp.where(ca == a, i1[a:a + 1, :], e1)
            e2 = jnp.where(cbi == a, i2[a:a + 1, :], e2)
        p = jnp.exp(sc - jnp.max(sc, axis=0, keepdims=True))
        gate = p / jnp.sum(p, axis=0, keepdims=True)
        rows = pl.ds(pl.multiple_of(head * PEER_TOPK, PEER_TOPK), PEER_TOPK)
        idx_t_ref[slot, rows, :] = e1 * PEER_N_KEYS + e2
        gate_t_ref[slot, rows, :] = gate


def _pack_pairs(x):
    half = x.shape[1] // 2
    bits = lax.bitcast_convert_type(x.astype(BF16).astype(F32), U32)
    return (bits[:, :half] >> 16) | (bits[:, half:] & jnp.uint32(0xFFFF0000))


def _unpack_pairs(w):
    lo = lax.bitcast_convert_type(w << 16, F32)
    hi = lax.bitcast_convert_type(w & jnp.uint32(0xFFFF0000), F32)
    return lo, hi


def _pack_tables_body(u_ref, v_ref, o_ref):
    half_chunks = o_ref.shape[1] // 2
    for t, tab_ref in enumerate((u_ref, v_ref)):
        packed = _pack_pairs(tab_ref[...])
        for c in range(half_chunks):
            o_ref[:, t * half_chunks + c, 0, :] = packed[:, c * LANES:(c + 1) * LANES]


def pack_tables(u_tabs, v_tabs, layer, tm=256):
    _, n, d = u_tabs.shape
    n_chunks = d // LANES
    layer_rows = pl.BlockSpec((None, tm, d), lambda i: (layer, i, 0))
    return pl.pallas_call(
        _pack_tables_body,
        grid=(n // tm,),
        in_specs=[layer_rows, layer_rows],
        out_specs=pl.BlockSpec((tm, n_chunks, 1, LANES), lambda i: (i, 0, 0, 0)),
        out_shape=jax.ShapeDtypeStruct((n, n_chunks, 1, LANES), U32),
        compiler_params=_params("arbitrary"),
        name="peer_pack_tables",
    )(u_tabs, v_tabs)


def _gelu(x):
    return 0.5 * x * (1.0 + lax.erf(x * (1.0 / math.sqrt(2.0))))


def _peer_body(h0_ref, hn_ref, wq_ref, k1_ref, k2_ref, h_ref, ng_ref, pg_ref, tab_hbm, o_ref, *rest,
               tb, n_blocks, post_gain_of_out):
    n_post = len(post_gain_of_out)
    post_refs = rest[:n_post]
    (xb_next, xn_rows, q_ring, idx_t_ref, gate_t_ref, idx_rows, gate_rows, idx_smem, buf, mix_ref, idx_sem,
     row_sem) = rest[n_post:n_post + 12]
    route_scratch = rest[n_post + 12:]
    step = pl.program_id(0)
    blk = step - PEER_LEAD
    mixing = blk >= 0
    n_groups = tb // SUBLANES
    route_slot = lax.rem(step, PEER_RING)
    q_cur = q_ring.at[lax.rem(step, 2)]
    q_next = q_ring.at[lax.rem(step + 1, 2)]
    kc_w = wq_ref.shape[1] // (2 * ROUTE_KINDS)

    def normed(rows):
        return rows * lax.rsqrt(jnp.mean(rows * rows, axis=-1, keepdims=True) + NORM_EPS) * ng_ref[...]

    @pl.when(step == 0)
    def _():
        xb0 = normed(h0_ref[...]).astype(BF16)
        for hd in range(PEER_HEADS):
            q_cur[hd] = jnp.dot(xb0, wq_ref[hd], preferred_element_type=F32)

    xb_next[...] = normed(hn_ref[...]).astype(BF16)

    def project(kind, head, u):
        kc = 2 * kind + u
        ks = slice(kc * kc_w, (kc + 1) * kc_w)
        part = jnp.dot(xb_next[:, ks], wq_ref[head, ks, :], preferred_element_type=F32)
        if kc == 0:
            q_next[head] = part
        else:
            q_next[head] += part

    def route_step(kind, e, head):
        _route_step(kind, e, head, route_slot, q_cur, k1_ref, k2_ref, idx_t_ref, gate_t_ref, *route_scratch)

    def route_piece(kind, head):
        for u in range(2):
            project(kind, head, u)
        for e in range(PEER_TOPK):
            route_step(kind, e, head)

    def row_copy(j, slot, k):
        e = idx_smem[j, k]
        return pltpu.make_async_copy(tab_hbm.at[e], buf.at[slot, :, pl.ds(k, 1), :], row_sem.at[slot])

    def wait(slot):
        pltpu.make_async_copy(buf.at[slot], buf.at[slot], row_sem.at[slot]).wait()

    @pl.when(mixing)
    def _():
        slot = lax.rem(blk, PEER_RING)
        nxt = lax.rem(blk + 1, PEER_RING)
        for j in range(tb // LANES):
            cols = slice(j * LANES, (j + 1) * LANES)
            gate_rows[cols, :] = gate_t_ref[slot, :, cols].T
            idx_rows[cols, :] = idx_t_ref[slot, :, cols].astype(F32).T.astype(jnp.int32)
        head_rows = idx_t_ref[nxt, :, 0:LANES].astype(F32).T.astype(jnp.int32)
        idx_rows[tb:tb + PEER_BUFS, :] = head_rows[0:PEER_BUFS, :]
        ids = pltpu.make_async_copy(idx_rows, idx_smem, idx_sem)
        ids.start()
        ids.wait()

    @pl.when(blk == 0)
    def _():
        for j in range(PEER_AHEAD):
            def body(k, carry):
                row_copy(j, j, k).start()
                return carry
            lax.fori_loop(0, PEER_SLOTS, body, 0, unroll=8)

    d = h_ref.shape[1]
    half_chunks = d // (2 * LANES)
    row = lax.broadcasted_iota(jnp.int32, (PEER_SLOTS, LANES), 0)
    col = lax.broadcasted_iota(jnp.int32, (PEER_SLOTS, LANES), 1)
    eye = row == col
    per_chunk = PEER_SLOTS // (2 * half_chunks)

    def mix_group(g, kind):
        base = pl.multiple_of(g * SUBLANES, SUBLANES)
        rows = pl.ds(base, SUBLANES)
        gate8 = gate_rows[rows, :]
        xn_rows[...] = normed(h_ref[rows, :])
        head = g // ROUTE_KINDS
        steps_per_token = PEER_TOPK // SUBLANES

        for r in range(SUBLANES):
            ahead_slot = (r + PEER_AHEAD) % PEER_BUFS
            started = [0]

            def start_some():
                for k in range(started[0], started[0] + per_chunk):
                    row_copy(base + r + PEER_AHEAD, ahead_slot, k).start(priority=k % 2)
                started[0] += per_chunk

            wait(r)
            acc = jnp.zeros((PEER_SLOTS, LANES), F32)
            for c in range(half_chunks):
                lo, hi = _unpack_pairs(buf[r, c])
                x_lo = xn_rows[r:r + 1, c * LANES:(c + 1) * LANES]
                x_hi = xn_rows[r:r + 1, (half_chunks + c) * LANES:(half_chunks + c + 1) * LANES]
                acc = acc + lo * x_lo + hi * x_hi
                start_some()
                if c == 0 and r in (2, 5):
                    project(kind, head, r // 3)
                if c == half_chunks // 2:
                    route_step(kind, steps_per_token * r, head)
            hid = jnp.sum(acc, axis=1, keepdims=True)
            gate_row = jnp.broadcast_to(gate8[r:r + 1, :], (PEER_SLOTS, LANES))
            gate = jnp.sum(jnp.where(eye, gate_row, 0.0), axis=1, keepdims=True)
            w = jnp.broadcast_to(gate * _gelu(hid), (PEER_SLOTS, LANES))
            for c in range(half_chunks):
                lo, hi = _unpack_pairs(buf[r, half_chunks + c])
                mix_ref[r:r + 1, c * LANES:(c + 1) * LANES] = jnp.sum(lo * w, axis=0, keepdims=True)
                mix_ref[r:r + 1, (half_chunks + c) * LANES:(half_chunks + c + 1) * LANES] = (
                    jnp.sum(hi * w, axis=0, keepdims=True))
                start_some()
                if c == half_chunks // 2:
                    for e in range(steps_per_token * r + 1, steps_per_token * (r + 1)):
                        route_step(kind, e, head)
        y = h_ref[rows, :] + mix_ref[...]
        o_ref[rows, :] = y
        if post_refs:
            yn = y * lax.rsqrt(jnp.mean(y * y, axis=-1, keepdims=True) + NORM_EPS)
            for p_ref, gi in zip(post_refs, post_gain_of_out):
                p_ref[rows, :] = (yn * pg_ref[gi:gi + 1, :]).astype(p_ref.dtype)

    def group(g, carry):
        kind = lax.rem(g, ROUTE_KINDS)

        @pl.when(mixing)
        def _():
            lax.switch(kind, [functools.partial(mix_group, kind=k) for k in range(ROUTE_KINDS)], g)

        @pl.when(jnp.logical_not(mixing))
        def _():
            lax.switch(kind, [functools.partial(route_piece, k) for k in range(ROUTE_KINDS)], g // ROUTE_KINDS)
        return carry

    lax.fori_loop(0, n_groups, group, 0)

    @pl.when(blk == n_blocks - 1)
    def _():
        for j in range(PEER_AHEAD):
            wait(j)


def peer_layer(h, norm_g, w_q, sub_keys, table, post_gains, post_outs, tb=256):
    t, d = h.shape
    n_blocks = t // tb
    n_chunks = d // LANES
    qw = 2 * PEER_HALF
    assert PEER_BUFS == SUBLANES and PEER_AHEAD < PEER_BUFS and PEER_SLOTS % n_chunks == 0
    assert tb % LANES == 0 and tb // SUBLANES == ROUTE_PIECES and n_blocks * tb == t
    assert table.shape[1:] == (n_chunks, 1, LANES) and w_q.shape == (d, PEER_HEADS * qw)
    wq_heads = w_q.astype(BF16).reshape(d, PEER_HEADS, qw).transpose(1, 0, 2)
    pg = jnp.stack(post_gains)
    mixed = lambda s: (jnp.maximum(s - PEER_LEAD, 0), 0)
    keys = pl.BlockSpec((PEER_N_KEYS, PEER_HALF), lambda s: (0, 0))
    row_blk = pl.BlockSpec((tb, d), mixed)
    return pl.pallas_call(
        functools.partial(_peer_body, tb=tb, n_blocks=n_blocks, post_gain_of_out=tuple(gi for gi, _ in post_outs)),
        grid=(n_blocks + PEER_LEAD,),
        in_specs=[
            pl.BlockSpec((tb, d), lambda s: (0, 0)),
            pl.BlockSpec((tb, d), lambda s: (jnp.minimum(s + 1, n_blocks - 1), 0)),
            pl.BlockSpec((PEER_HEADS, d, qw), lambda s: (0, 0, 0)),
            keys, keys,
            row_blk,
            pl.BlockSpec((1, d), lambda s: (0, 0)),
            pl.BlockSpec(pg.shape, lambda s: (0, 0)),
            pl.BlockSpec(memory_space=pl.ANY),
        ],
        out_specs=[row_blk] + [row_blk for _ in post_outs],
        out_shape=[jax.ShapeDtypeStruct((t, d), F32)] + [jax.ShapeDtypeStruct((t, d), dt) for _, dt in post_outs],
        scratch_shapes=[
            pltpu.VMEM((tb, d), BF16),
            pltpu.VMEM((SUBLANES, d), F32),
            pltpu.VMEM((2, PEER_HEADS, tb, qw), F32),
            pltpu.VMEM((PEER_RING, PEER_SLOTS, tb), jnp.int32),
            pltpu.VMEM((PEER_RING, PEER_SLOTS, tb), F32),
            pltpu.VMEM((tb + PEER_BUFS, PEER_SLOTS), jnp.int32),
            pltpu.VMEM((tb, PEER_SLOTS), F32),
            pltpu.SMEM((tb + PEER_BUFS, PEER_SLOTS), jnp.int32),
            pltpu.VMEM((PEER_BUFS, n_chunks, PEER_SLOTS, LANES), U32),
            pltpu.VMEM((SUBLANES, d), F32),
            pltpu.SemaphoreType.DMA(()),
            pltpu.SemaphoreType.DMA((PEER_BUFS,)),
            pltpu.VMEM((2, PEER_N_KEYS, tb), F32),
            pltpu.VMEM((2, PEER_TOPK, tb), F32), pltpu.VMEM((2, PEER_TOPK, tb), F32),
            pltpu.VMEM((PAIR_ROWS, tb), F32), pltpu.VMEM((PAIR_ROWS, tb), F32),
            pltpu.VMEM((PEER_TOPK, tb), F32), pltpu.VMEM((PEER_TOPK, tb), F32),
        ],
        compiler_params=pltpu.CompilerParams(dimension_semantics=("arbitrary",), vmem_limit_bytes=PEER_VMEM_LIMIT),
        name="peer_route_mix",
    )(h, h, wq_heads, sub_keys[0], sub_keys[1], h, norm_g.reshape(1, d), pg, table)


def kernel(x, norm_mix, norm_ffn, norm_final, m_in_proj, m_conv_w, m_conv_b, m_dt_bias, m_a_log, m_d, m_norm,
           m_out_proj, kv_norm, w_kv, sb_w_q, sb_w_o, peer_w_q, peer_sub_keys, peer_u, peer_v):
    batch, seq, d_model = x.shape
    t = batch * seq
    h = x.reshape(t, d_model)

    n_heads = m_dt_bias.shape[1]
    proj_w = m_in_proj.shape[2] - n_heads
    u, = rmsnorm(h, [norm_mix[0]], [(0, BF16)])
    w_in = m_in_proj[0]
    zx = matmul(u, w_in, n=proj_w, name="in_proj")
    w_dt = jnp.pad(w_in[:, proj_w:], ((0, 0), (0, LANES - n_heads)))
    dtr = matmul(u, w_dt, name="in_proj_dt")
    y = mamba_ssd(zx, dtr, m_conv_w[0], m_conv_b[0], m_dt_bias[0], m_a_log[0], m_d[0], m_norm[0], batch, seq)
    h = matmul(y, m_out_proj[0], residual=h, tn=512, name="out_proj")
    h, kvn, qn = peer_layer(h, norm_ffn[0], peer_w_q[0], peer_sub_keys[0], pack_tables(peer_u, peer_v, 0),
                            [kv_norm, norm_mix[1]], [(0, BF16), (1, BF16)])

    kv = matmul(kvn, w_kv, out_dtype=BF16, name="kv_proj")
    q = matmul(qn, sb_w_q[0], name="q_proj")
    o = stick_breaking_attention(q, kv, batch, seq)
    h = matmul(o, sb_w_o[0], residual=h, name="o_proj")
    _, out = peer_layer(h, norm_ffn[1], peer_w_q[1], peer_sub_keys[1], pack_tables(peer_u, peer_v, 1),
                        [norm_final], [(0, F32)])
    return out.reshape(batch, seq, d_model)
```

```python
import functools
import math

import jax
import jax.numpy as jnp
from jax import lax
from jax.experimental import pallas as pl
from jax.experimental.pallas import tpu as pltpu

F32 = jnp.float32
BF16 = jnp.bfloat16
U32 = jnp.uint32
HIGHEST = lax.Precision.HIGHEST

NORM_EPS = 1e-6
LANES = 128
SUBLANES = 8

M_HEAD_DIM = 64
M_D_STATE = 128
M_GROUPS = 8
M_D_CONV = 4
M_CHUNK = 128
M_NORM_EPS = 1e-5

SB_HEAD_DIM = 128
SB_BLOCK = 128
SB_Q_TILE = 512
SB_HEADS_PER_STEP = 8

PEER_HEADS = 8
PEER_N_KEYS = 128
PEER_TOPK = 16
PEER_HALF = 128
PEER_SLOTS = PEER_HEADS * PEER_TOPK
PEER_BUFS = 8
PEER_AHEAD = 6
PEER_LEAD = 2
PEER_RING = PEER_LEAD + 1

VMEM_LIMIT = 48 * 1024 * 1024
PEER_VMEM_LIMIT = 56 * 1024 * 1024


def _params(*sem):
    return pltpu.CompilerParams(dimension_semantics=sem, vmem_limit_bytes=VMEM_LIMIT)


def _rmsnorm_body(x_ref, g_ref, *o_refs, eps, gain_of_out):
    x = x_ref[...]
    xn = x * lax.rsqrt(jnp.mean(x * x, axis=-1, keepdims=True) + eps)
    for o_ref, gi in zip(o_refs, gain_of_out):
        o_ref[...] = (xn * g_ref[gi:gi + 1, :]).astype(o_ref.dtype)


def _split_bf16(x):
    hi = x.astype(BF16)
    return hi, (x - hi.astype(F32)).astype(BF16)


def rmsnorm(x, gains, outs, tm=256):
    t, d = x.shape
    g = jnp.stack(gains)
    blk = pl.BlockSpec((tm, d), lambda i: (i, 0))
    return pl.pallas_call(
        functools.partial(_rmsnorm_body, eps=NORM_EPS, gain_of_out=tuple(gi for gi, _ in outs)),
        grid=(t // tm,),
        in_specs=[blk, pl.BlockSpec(g.shape, lambda i: (0, 0))],
        out_specs=[blk for _ in outs],
        out_shape=[jax.ShapeDtypeStruct((t, d), dt) for _, dt in outs],
        compiler_params=_params("arbitrary"),
        name="rmsnorm",
    )(x, g)


def _mm_body(a_ref, w_ref, *rest):
    o_ref, wb_ref = rest[-2:]

    @pl.when(pl.program_id(1) == 0)
    def _():
        wb_ref[...] = w_ref[...].astype(BF16)

    acc = jnp.dot(a_ref[...], wb_ref[...], preferred_element_type=F32)
    if len(rest) == 3:
        acc = rest[0][...] + acc
    o_ref[...] = acc.astype(o_ref.dtype)


def matmul(a, w, n=None, residual=None, out_dtype=F32, tm=1024, tn=1024, by_column_block=False, name="matmul"):
    m, k = a.shape
    n = w.shape[1] if n is None else n
    tn = min(tn, n)
    assert m % tm == 0 and n % tn == 0
    in_specs = [pl.BlockSpec((tm, k), lambda j, i: (i, 0)), pl.BlockSpec((k, tn), lambda j, i: (0, j))]
    args = [a, w]
    if residual is not None:
        in_specs.append(pl.BlockSpec((tm, tn), lambda j, i: (i, j)))
        args.append(residual)
    if by_column_block:
        out_spec = pl.BlockSpec((None, tm, tn), lambda j, i: (j, i, 0))
        out_shape = (n // tn, m, tn)
    else:
        out_spec = pl.BlockSpec((tm, tn), lambda j, i: (i, j))
        out_shape = (m, n)
    return pl.pallas_call(
        _mm_body,
        grid=(n // tn, m // tm),
        in_specs=in_specs,
        out_specs=out_spec,
        out_shape=jax.ShapeDtypeStruct(out_shape, out_dtype),
        scratch_shapes=[pltpu.VMEM((k, tn), BF16)],
        compiler_params=_params("arbitrary", "arbitrary"),
        name=name,
    )(*args)


def _softplus(x):
    return jnp.maximum(x, 0.0) + jnp.log1p(jnp.exp(-jnp.abs(x)))


def _silu(x):
    return x * jax.nn.sigmoid(x)


def _causal_conv_silu(raw_ref, prev_ref, w_ref, bias_ref, out_ref):
    cur = raw_ref[...]
    prev8 = prev_ref[...]
    w = w_ref[...]
    bias = bias_ref[...]
    top = cur[0:SUBLANES, :]
    row8 = lax.broadcasted_iota(jnp.int32, top.shape, 0)
    acc = cur * w[M_D_CONV - 1:M_D_CONV, :]
    acc_top = top * w[M_D_CONV - 1:M_D_CONV, :]
    for d in range(1, M_D_CONV):
        wd = w[M_D_CONV - 1 - d:M_D_CONV - d, :]
        shifted = pltpu.roll(cur, d, 0)
        acc = acc + shifted * wd
        shifted_top = jnp.where(row8 < d, pltpu.roll(prev8, d, 0), shifted[0:SUBLANES, :])
        acc_top = acc_top + shifted_top * wd
    out_ref[...] = _silu(acc + bias)
    out_ref[0:SUBLANES, :] = _silu(acc_top + bias)
    prev_ref[...] = cur[M_CHUNK - SUBLANES:M_CHUNK, :]


def _mamba_body(z_ref, x_ref, b_ref, c_ref, dtr_ref, cwx_ref, cwb_ref, cwc_ref, cbx_ref, cbb_ref, cbc_ref,
                dtb_ref, alog_ref, dexp_ref, nw_ref, e_ref, o_ref,
                state_ref, px_ref, pb_ref, pc_ref, xc_ref, bc_ref, cc_ref, y_ref):
    @pl.when(pl.program_id(1) == 0)
    def _():
        state_ref[...] = jnp.zeros_like(state_ref)
        px_ref[...] = jnp.zeros_like(px_ref)
        pb_ref[...] = jnp.zeros_like(pb_ref)
        pc_ref[...] = jnp.zeros_like(pc_ref)

    _causal_conv_silu(x_ref, px_ref, cwx_ref, cbx_ref, xc_ref)
    _causal_conv_silu(b_ref, pb_ref, cwb_ref, cbb_ref, bc_ref)
    _causal_conv_silu(c_ref, pc_ref, cwc_ref, cbc_ref, cc_ref)

    L = M_CHUNK
    row = lax.broadcasted_iota(jnp.int32, (L, L), 0)
    col = lax.broadcasted_iota(jnp.int32, (L, L), 1)
    causal = col <= row
    ltri = causal.astype(F32)

    dt = _softplus(dtr_ref[...] + dtb_ref[...])
    dta = dt * (-jnp.exp(alog_ref[...]))
    acs = jnp.dot(ltri, dta, preferred_element_type=F32, precision=HIGHEST)
    dt_t = dt.T
    acs_t = acs.T
    e = e_ref[...]

    def expand(v):
        hi, rest = v.astype(BF16), v
        rest = rest - hi.astype(F32)
        mid = rest.astype(BF16)
        lo = (rest - mid.astype(F32)).astype(BF16)
        return (jnp.dot(hi, e, preferred_element_type=F32) + jnp.dot(mid, e, preferred_element_type=F32)
                + jnp.dot(lo, e, preferred_element_type=F32))

    dt_exp = expand(dt)
    acs_exp = expand(acs)

    gw = M_HEAD_DIM * 8
    first_half = col < M_HEAD_DIM
    for g in range(M_GROUPS):
        bg = bc_ref[:, g * M_D_STATE:(g + 1) * M_D_STATE].astype(BF16)
        cg = cc_ref[:, g * M_D_STATE:(g + 1) * M_D_STATE].astype(BF16)
        xg = xc_ref[:, g * gw:(g + 1) * gw]
        ae = acs_exp[:, g * gw:(g + 1) * gw]
        de = dt_exp[:, g * gw:(g + 1) * gw]
        cb = lax.dot_general(cg, bg, (((1,), (1,)), ((), ())), preferred_element_type=F32)
        sg = state_ref[g]
        y_off = jnp.dot(cg, sg.astype(BF16), preferred_element_type=F32) * jnp.exp(ae)
        a_last = ae[L - 1:L, :]
        to_end = jnp.exp(a_last - ae) * de
        xw = (xg * to_end).astype(BF16)
        upd = lax.dot_general(bg, xw, (((0,), (0,)), ((), ())), preferred_element_type=F32)
        state_ref[g] = sg * jnp.exp(a_last) + upd
        for jp in range(4):
            xpair = xg[:, jp * LANES:(jp + 1) * LANES]
            ys = jnp.zeros((L, LANES), F32)
            for half in range(2):
                h = g * 8 + jp * 2 + half
                a_col = jnp.sum(jnp.where(col == h, acs, 0.0), axis=1, keepdims=True)
                a_row = acs_t[h:h + 1, :]
                d_row = dt_t[h:h + 1, :]
                decay = jnp.exp(jnp.where(causal, a_col - a_row, -jnp.inf))
                w = (cb * decay * d_row).astype(BF16)
                keep = first_half if half == 0 else jnp.logical_not(first_half)
                xm = jnp.where(keep, xpair, 0.0).astype(BF16)
                ys = ys + jnp.dot(w, xm, preferred_element_type=F32)
            lo = g * gw + jp * LANES
            y_ref[:, lo:lo + LANES] = (ys + y_off[:, jp * LANES:(jp + 1) * LANES]
                                       + xpair * dexp_ref[:, lo:lo + LANES])

    for g in range(M_GROUPS):
        sl = slice(g * gw, (g + 1) * gw)
        yf = y_ref[:, sl] * _silu(z_ref[:, sl])
        ms = jnp.mean(yf * yf, axis=1, keepdims=True)
        o_ref[:, sl] = (yf * lax.rsqrt(ms + M_NORM_EPS) * nw_ref[:, sl]).astype(o_ref.dtype)


def mamba_ssd(zx, dtr, conv_w, conv_b, dt_bias, a_log, d_skip, norm_w, batch, seq):
    t = zx.shape[0]
    d_inner = norm_w.shape[0]
    n_heads = d_inner // M_HEAD_DIM
    bc_w = M_GROUPS * M_D_STATE
    nc = seq // M_CHUNK
    assert d_inner % 2048 == 0 and bc_w == 1024 and zx.shape[1] == 2 * d_inner + 2 * bc_w

    pad_h = LANES - n_heads
    dtb = jnp.pad(dt_bias, (0, pad_h)).reshape(1, LANES)
    alog = jnp.pad(a_log, (0, pad_h)).reshape(1, LANES)
    dexp = jnp.repeat(d_skip, M_HEAD_DIM).reshape(1, d_inner)
    expand = (jnp.arange(LANES)[:, None] == (jnp.arange(d_inner)[None, :] // M_HEAD_DIM)).astype(BF16)
    cwx, cwb, cwc = conv_w[:, :d_inner], conv_w[:, d_inner:d_inner + bc_w], conv_w[:, d_inner + bc_w:]
    cb2 = conv_b.reshape(1, -1)
    cbx, cbb, cbc = cb2[:, :d_inner], cb2[:, d_inner:d_inner + bc_w], cb2[:, d_inner + bc_w:]

    row_map = lambda b, c: (b * nc + c, 0)
    const = lambda b, c: (0, 0)
    xblk = d_inner // bc_w
    in_specs = [
        pl.BlockSpec((M_CHUNK, d_inner), row_map),
        pl.BlockSpec((M_CHUNK, d_inner), lambda b, c: (b * nc + c, 1)),
        pl.BlockSpec((M_CHUNK, bc_w), lambda b, c: (b * nc + c, 2 * xblk)),
        pl.BlockSpec((M_CHUNK, bc_w), lambda b, c: (b * nc + c, 2 * xblk + 1)),
        pl.BlockSpec((M_CHUNK, LANES), row_map),
        pl.BlockSpec((M_D_CONV, d_inner), const), pl.BlockSpec((M_D_CONV, bc_w), const),
        pl.BlockSpec((M_D_CONV, bc_w), const),
        pl.BlockSpec((1, d_inner), const), pl.BlockSpec((1, bc_w), const), pl.BlockSpec((1, bc_w), const),
        pl.BlockSpec((1, LANES), const), pl.BlockSpec((1, LANES), const),
        pl.BlockSpec((1, d_inner), const), pl.BlockSpec((1, d_inner), const),
        pl.BlockSpec((LANES, d_inner), const),
    ]
    return pl.pallas_call(
        _mamba_body,
        grid=(batch, nc),
        in_specs=in_specs,
        out_specs=pl.BlockSpec((M_CHUNK, d_inner), row_map),
        out_shape=jax.ShapeDtypeStruct((t, d_inner), BF16),
        scratch_shapes=[
            pltpu.VMEM((M_GROUPS, M_D_STATE, M_HEAD_DIM * 8), F32),
            pltpu.VMEM((SUBLANES, d_inner), F32), pltpu.VMEM((SUBLANES, bc_w), F32),
            pltpu.VMEM((SUBLANES, bc_w), F32),
            pltpu.VMEM((M_CHUNK, d_inner), F32), pltpu.VMEM((M_CHUNK, bc_w), F32),
            pltpu.VMEM((M_CHUNK, bc_w), F32),
            pltpu.VMEM((M_CHUNK, d_inner), F32),
        ],
        compiler_params=_params("arbitrary", "arbitrary"),
        name="mamba_ssd",
    )(zx, zx, zx, zx, dtr, cwx, cwb, cwc, cbx, cbb, cbc, dtb, alog, dexp, norm_w.reshape(1, d_inner), expand)


def _attn_body(q_ref, k_ref, v_ref, o_ref, *scratch, scale):
    acc_refs = scratch[:SB_HEADS_PER_STEP]
    tail_refs = scratch[SB_HEADS_PER_STEP:]
    qt = pl.program_id(2)
    tq, blk, hd = SB_Q_TILE, SB_BLOCK, SB_HEAD_DIM
    per_tile = tq // blk
    row = lax.broadcasted_iota(jnp.int32, (tq, blk), 0)
    col = lax.broadcasted_iota(jnp.int32, (tq, blk), 1)
    r2 = lax.broadcasted_iota(jnp.int32, (blk, 2 * blk), 0)
    c2 = lax.broadcasted_iota(jnp.int32, (blk, 2 * blk), 1)
    sums = ((r2 > c2) | (c2 >= blk)).astype(BF16)
    for ref in scratch:
        ref[...] = jnp.zeros_like(ref)

    def make_step(first_kb, masked):
        def step(i, carry):
            kb = first_kb - i
            start = pl.multiple_of(kb * blk, blk)
            if masked:
                strict = (kb * blk + col) < (qt * tq + row)
            heads = range(SB_HEADS_PER_STEP)
            lanes = [slice(h * hd, (h + 1) * hd) for h in heads]
            z = [lax.dot_general((q_ref[:, lanes[h]] * scale).astype(BF16), k_ref[pl.ds(start, blk), lanes[h]],
                                 (((1,), (1,)), ((), ())), preferred_element_type=F32) for h in heads]
            log_1mb, log_b, s = [], [], []
            for h in heads:
                lp = jnp.log(1.0 + jnp.exp(-jnp.abs(z[h])))
                l1 = -(jnp.maximum(z[h], 0.0) + lp)
                log_b.append(z[h] + l1)
                log_1mb.append(jnp.where(strict, l1, 0.0) if masked else l1)
            for h in heads:
                hi, lo = _split_bf16(log_1mb[h])
                s.append(jnp.dot(hi, sums, preferred_element_type=F32)
                         + jnp.dot(lo, sums, preferred_element_type=F32))
            for h in heads:
                tail = tail_refs[h][...]
                a = jnp.exp(log_b[h] + s[h][:, :blk] + tail)
                if masked:
                    a = jnp.where(strict, a, 0.0)
                acc_refs[h][...] += jnp.dot(a.astype(BF16), v_ref[pl.ds(start, blk), lanes[h]],
                                            preferred_element_type=F32)
                tail_refs[h][...] = tail + s[h][:, blk:]
            return carry
        return step

    lax.fori_loop(0, per_tile, make_step(qt * per_tile + per_tile - 1, True), 0)
    lax.fori_loop(0, qt * per_tile, make_step(qt * per_tile - 1, False), 0)
    for h in range(SB_HEADS_PER_STEP):
        o_ref[:, h * hd:(h + 1) * hd] = acc_refs[h][...].astype(o_ref.dtype)


def stick_breaking_attention(q, kv, batch, seq):
    t, width = q.shape
    nq = seq // SB_Q_TILE
    gw = SB_HEADS_PER_STEP * SB_HEAD_DIM
    n_groups = width // gw
    return pl.pallas_call(
        functools.partial(_attn_body, scale=1.0 / math.sqrt(SB_HEAD_DIM)),
        grid=(batch, n_groups, nq),
        in_specs=[
            pl.BlockSpec((SB_Q_TILE, gw), lambda b, g, i: (b * nq + i, g)),
            pl.BlockSpec((seq, gw), lambda b, g, i: (b, g)),
            pl.BlockSpec((seq, gw), lambda b, g, i: (b, n_groups + g)),
        ],
        out_specs=pl.BlockSpec((SB_Q_TILE, gw), lambda b, g, i: (b * nq + i, g)),
        out_shape=jax.ShapeDtypeStruct((t, width), BF16),
        scratch_shapes=([pltpu.VMEM((SB_Q_TILE, SB_HEAD_DIM), F32)] * SB_HEADS_PER_STEP
                        + [pltpu.VMEM((SB_Q_TILE, SB_BLOCK), F32)] * SB_HEADS_PER_STEP),
        compiler_params=_params("arbitrary", "arbitrary", "arbitrary"),
        name="stick_breaking_attention",
    )(q, kv, kv)


def _take_best(s_ref, ids, v_ref, i_ref, e):
    s = s_ref[...]
    m = jnp.max(s, axis=0, keepdims=True)
    first = jnp.min(jnp.where(s == m, ids, jnp.inf), axis=0, keepdims=True)
    s_ref[...] = jnp.where(ids == first, -jnp.inf, s)
    v_ref[e:e + 1, :] = m
    i_ref[e:e + 1, :] = first


def _pair_candidates(v1, v2):
    k = PEER_TOPK
    tokens = v1.shape[1]
    sub = lax.broadcasted_iota(jnp.int32, (SUBLANES, tokens), 0)
    sub16 = lax.broadcasted_iota(jnp.int32, (k, tokens), 0)
    vals, ids = [], []
    vals.append(v1[0:1, :] + v2)
    ids.append(sub16)
    vals.append(jnp.where(sub16 >= 1, v1 + v2[0:1, :], -jnp.inf))
    ids.append(sub16 * k)
    for a in range(1, 5):
        b_max = k // (a + 1) - 1
        vals.append(jnp.where((sub >= 1) & (sub <= b_max), v1[a:a + 1, :] + v2[0:SUBLANES, :], -jnp.inf))
        ids.append(sub + a * k)
    vals.append(jnp.where((sub >= 5) & (sub <= 7), v1[0:SUBLANES, :] + v2[1:2, :], -jnp.inf))
    ids.append(sub * k + 1)
    return jnp.concatenate(vals, axis=0), jnp.concatenate(ids, axis=0).astype(F32)


PAIR_ROWS = 2 * PEER_TOPK + 5 * SUBLANES
ROUTE_KINDS = 4
ROUTE_PIECES = ROUTE_KINDS * PEER_HEADS


def _route_step(kind, e, head, slot, q_ref, k1_ref, k2_ref, idx_t_ref, gate_t_ref,
                s_ref, v_ref, i_ref, cand_ref, cid_ref, sc_ref, ci_ref):
    if kind == 0:
        if e == 0:
            nt = (((1,), (1,)), ((), ()))
            q = q_ref[head]
            s_ref[0] = lax.dot_general(k1_ref[...], q[:, :PEER_HALF], nt, preferred_element_type=F32,
                                       precision=HIGHEST)
            s_ref[1] = lax.dot_general(k2_ref[...], q[:, PEER_HALF:], nt, preferred_element_type=F32,
                                       precision=HIGHEST)
        return
    if kind in (1, 2):
        half = kind - 1
        key_ids = lax.broadcasted_iota(jnp.int32, s_ref.shape[1:], 0).astype(F32)
        _take_best(s_ref.at[half], key_ids, v_ref.at[half], i_ref.at[half], e)
        return
    if e == 0:
        cand_ref[...], cid_ref[...] = _pair_candidates(v_ref[0], v_ref[1])
    _take_best(cand_ref, cid_ref[...], sc_ref, ci_ref, e)
    if e == PEER_TOPK - 1:
        sc = sc_ref[...]
        ci = ci_ref[...].astype(jnp.int32)
        i1 = i_ref[0].astype(jnp.int32)
        i2 = i_ref[1].astype(jnp.int32)
        ca = ci >> 4
        cbi = ci & (PEER_TOPK - 1)
        e1 = jnp.zeros_like(ci)
        e2 = jnp.zeros_like(ci)
        for a in range(PEER_TOPK):
            e1 = jnp.where(ca == a, i1[a:a + 1, :], e1)
            e2 = jnp.where(cbi == a, i2[a:a + 1, :], e2)
        p = jnp.exp(sc - jnp.max(sc, axis=0, keepdims=True))
        gate = p / jnp.sum(p, axis=0, keepdims=True)
        rows = pl.ds(pl.multiple_of(head * PEER_TOPK, PEER_TOPK), PEER_TOPK)
        idx_t_ref[slot, rows, :] = e1 * PEER_N_KEYS + e2
        gate_t_ref[slot, rows, :] = gate


def _pack_pairs(x):
    half = x.shape[1] // 2
    bits = lax.bitcast_convert_type(x.astype(BF16).astype(F32), U32)
    return (bits[:, :half] >> 16) | (bits[:, half:] & jnp.uint32(0xFFFF0000))


def _unpack_pairs(w):
    lo = lax.bitcast_convert_type(w << 16, F32)
    hi = lax.bitcast_convert_type(w & jnp.uint32(0xFFFF0000), F32)
    return lo, hi


def _pack_tables_body(u_ref, v_ref, o_ref):
    half_chunks = o_ref.shape[1] // 2
    for t, tab_ref in enumerate((u_ref, v_ref)):
        packed = _pack_pairs(tab_ref[...])
        for c in range(half_chunks):
            o_ref[:, t * half_chunks + c, 0, :] = packed[:, c * LANES:(c + 1) * LANES]


def pack_tables(u_tabs, v_tabs, layer, tm=256):
    _, n, d = u_tabs.shape
    n_chunks = d // LANES
    layer_rows = pl.BlockSpec((None, tm, d), lambda i: (layer, i, 0))
    return pl.pallas_call(
        _pack_tables_body,
        grid=(n // tm,),
        in_specs=[layer_rows, layer_rows],
        out_specs=pl.BlockSpec((tm, n_chunks, 1, LANES), lambda i: (i, 0, 0, 0)),
        out_shape=jax.ShapeDtypeStruct((n, n_chunks, 1, LANES), U32),
        compiler_params=_params("arbitrary"),
        name="peer_pack_tables",
    )(u_tabs, v_tabs)


def _gelu(x):
    return 0.5 * x * (1.0 + lax.erf(x * (1.0 / math.sqrt(2.0))))


def _peer_body(xb0_ref, xbn_ref, wq_ref, k1_ref, k2_ref, x_ref, h_ref, pg_ref, tab_hbm, o_ref, *rest,
               tb, n_blocks, post_gain_of_out):
    n_post = len(post_gain_of_out)
    post_refs = rest[:n_post]
    (q_ring, idx_t_ref, gate_t_ref, idx_rows, gate_rows, idx_smem, buf, mix_ref, idx_sem,
     row_sem) = rest[n_post:n_post + 10]
    route_scratch = rest[n_post + 10:]
    step = pl.program_id(0)
    blk = step - PEER_LEAD
    mixing = blk >= 0
    n_groups = tb // SUBLANES
    route_slot = lax.rem(step, PEER_RING)
    q_cur = q_ring.at[lax.rem(step, 2)]
    q_next = q_ring.at[lax.rem(step + 1, 2)]
    kc_w = wq_ref.shape[1] // (2 * ROUTE_KINDS)

    @pl.when(step == 0)
    def _():
        for hd in range(PEER_HEADS):
            q_cur[hd] = jnp.dot(xb0_ref[...], wq_ref[hd], preferred_element_type=F32)

    def project(kind, head, u):
        kc = 2 * kind + u
        ks = slice(kc * kc_w, (kc + 1) * kc_w)
        part = jnp.dot(xbn_ref[:, ks], wq_ref[head, ks, :], preferred_element_type=F32)
        if kc == 0:
            q_next[head] = part
        else:
            q_next[head] += part

    def route_step(kind, e, head):
        _route_step(kind, e, head, route_slot, q_cur, k1_ref, k2_ref, idx_t_ref, gate_t_ref, *route_scratch)

    def route_piece(kind, head):
        for u in range(2):
            project(kind, head, u)
        for e in range(PEER_TOPK):
            route_step(kind, e, head)

    def row_copy(j, slot, k):
        e = idx_smem[j, k]
        return pltpu.make_async_copy(tab_hbm.at[e], buf.at[slot, :, pl.ds(k, 1), :], row_sem.at[slot])

    def wait(slot):
        pltpu.make_async_copy(buf.at[slot], buf.at[slot], row_sem.at[slot]).wait()

    @pl.when(mixing)
    def _():
        slot = lax.rem(blk, PEER_RING)
        nxt = lax.rem(blk + 1, PEER_RING)
        for j in range(tb // LANES):
            cols = slice(j * LANES, (j + 1) * LANES)
            gate_rows[cols, :] = gate_t_ref[slot, :, cols].T
            idx_rows[cols, :] = idx_t_ref[slot, :, cols].astype(F32).T.astype(jnp.int32)
        head_rows = idx_t_ref[nxt, :, 0:LANES].astype(F32).T.astype(jnp.int32)
        idx_rows[tb:tb + PEER_BUFS, :] = head_rows[0:PEER_BUFS, :]
        ids = pltpu.make_async_copy(idx_rows, idx_smem, idx_sem)
        ids.start()
        ids.wait()

    @pl.when(blk == 0)
    def _():
        for j in range(PEER_AHEAD):
            def body(k, carry):
                row_copy(j, j, k).start()
                return carry
            lax.fori_loop(0, PEER_SLOTS, body, 0, unroll=8)

    d = x_ref.shape[1]
    half_chunks = d // (2 * LANES)
    row = lax.broadcasted_iota(jnp.int32, (PEER_SLOTS, LANES), 0)
    col = lax.broadcasted_iota(jnp.int32, (PEER_SLOTS, LANES), 1)
    eye = row == col
    per_chunk = PEER_SLOTS // (2 * half_chunks)

    def mix_group(g, kind):
        base = pl.multiple_of(g * SUBLANES, SUBLANES)
        rows = pl.ds(base, SUBLANES)
        gate8 = gate_rows[rows, :]
        head = g // ROUTE_KINDS
        steps_per_token = PEER_TOPK // SUBLANES

        for r in range(SUBLANES):
            ahead_slot = (r + PEER_AHEAD) % PEER_BUFS
            started = [0]

            def start_some():
                for k in range(started[0], started[0] + per_chunk):
                    row_copy(base + r + PEER_AHEAD, ahead_slot, k).start(priority=k % 2)
                started[0] += per_chunk

            wait(r)
            acc = jnp.zeros((PEER_SLOTS, LANES), F32)
            for c in range(half_chunks):
                lo, hi = _unpack_pairs(buf[r, c])
                x_lo = x_ref[rows, c * LANES:(c + 1) * LANES][r:r + 1, :]
                x_hi = x_ref[rows, (half_chunks + c) * LANES:(half_chunks + c + 1) * LANES][r:r + 1, :]
                acc = acc + lo * x_lo + hi * x_hi
                start_some()
                if c == 0 and r in (2, 5):
                    project(kind, head, r // 3)
                if c == half_chunks // 2:
                    route_step(kind, steps_per_token * r, head)
            hid = jnp.sum(acc, axis=1, keepdims=True)
            gate_row = jnp.broadcast_to(gate8[r:r + 1, :], (PEER_SLOTS, LANES))
            gate = jnp.sum(jnp.where(eye, gate_row, 0.0), axis=1, keepdims=True)
            w = jnp.broadcast_to(gate * _gelu(hid), (PEER_SLOTS, LANES))
            for c in range(half_chunks):
                lo, hi = _unpack_pairs(buf[r, half_chunks + c])
                mix_ref[r:r + 1, c * LANES:(c + 1) * LANES] = jnp.sum(lo * w, axis=0, keepdims=True)
                mix_ref[r:r + 1, (half_chunks + c) * LANES:(half_chunks + c + 1) * LANES] = (
                    jnp.sum(hi * w, axis=0, keepdims=True))
                start_some()
                if c == half_chunks // 2:
                    for e in range(steps_per_token * r + 1, steps_per_token * (r + 1)):
                        route_step(kind, e, head)
        y = h_ref[rows, :] + mix_ref[...]
        o_ref[rows, :] = y
        if post_refs:
            yn = y * lax.rsqrt(jnp.mean(y * y, axis=-1, keepdims=True) + NORM_EPS)
            for p_ref, gi in zip(post_refs, post_gain_of_out):
                p_ref[rows, :] = (yn * pg_ref[gi:gi + 1, :]).astype(p_ref.dtype)

    def group(g, carry):
        kind = lax.rem(g, ROUTE_KINDS)

        @pl.when(mixing)
        def _():
            lax.switch(kind, [functools.partial(mix_group, kind=k) for k in range(ROUTE_KINDS)], g)

        @pl.when(jnp.logical_not(mixing))
        def _():
            lax.switch(kind, [functools.partial(route_piece, k) for k in range(ROUTE_KINDS)], g // ROUTE_KINDS)
        return carry

    lax.fori_loop(0, n_groups, group, 0)

    @pl.when(blk == n_blocks - 1)
    def _():
        for j in range(PEER_AHEAD):
            wait(j)


def peer_route_and_mix(xn, xn_b, w_q, sub_keys, h, table, post_gains, post_outs, tb=256):
    t, d = xn.shape
    n_blocks = t // tb
    n_chunks = d // LANES
    qw = 2 * PEER_HALF
    assert PEER_BUFS == SUBLANES and PEER_AHEAD < PEER_BUFS and PEER_SLOTS % n_chunks == 0
    assert tb % LANES == 0 and tb // SUBLANES == ROUTE_PIECES and n_blocks * tb == t
    assert table.shape[1:] == (n_chunks, 1, LANES) and w_q.shape == (d, PEER_HEADS * qw)
    wq_heads = w_q.astype(BF16).reshape(d, PEER_HEADS, qw).transpose(1, 0, 2)
    pg = jnp.stack(post_gains)
    mixed = lambda s: (jnp.maximum(s - PEER_LEAD, 0), 0)
    keys = pl.BlockSpec((PEER_N_KEYS, PEER_HALF), lambda s: (0, 0))
    row_blk = pl.BlockSpec((tb, d), mixed)
    return pl.pallas_call(
        functools.partial(_peer_body, tb=tb, n_blocks=n_blocks, post_gain_of_out=tuple(gi for gi, _ in post_outs)),
        grid=(n_blocks + PEER_LEAD,),
        in_specs=[
            pl.BlockSpec((tb, d), lambda s: (0, 0)),
            pl.BlockSpec((tb, d), lambda s: (jnp.minimum(s + 1, n_blocks - 1), 0)),
            pl.BlockSpec((PEER_HEADS, d, qw), lambda s: (0, 0, 0)),
            keys, keys,
            row_blk,
            row_blk,
            pl.BlockSpec(pg.shape, lambda s: (0, 0)),
            pl.BlockSpec(memory_space=pl.ANY),
        ],
        out_specs=[row_blk] + [row_blk for _ in post_outs],
        out_shape=[jax.ShapeDtypeStruct((t, d), F32)] + [jax.ShapeDtypeStruct((t, d), dt) for _, dt in post_outs],
        scratch_shapes=[
            pltpu.VMEM((2, PEER_HEADS, tb, qw), F32),
            pltpu.VMEM((PEER_RING, PEER_SLOTS, tb), jnp.int32),
            pltpu.VMEM((PEER_RING, PEER_SLOTS, tb), F32),
            pltpu.VMEM((tb + PEER_BUFS, PEER_SLOTS), jnp.int32),
            pltpu.VMEM((tb, PEER_SLOTS), F32),
            pltpu.SMEM((tb + PEER_BUFS, PEER_SLOTS), jnp.int32),
            pltpu.VMEM((PEER_BUFS, n_chunks, PEER_SLOTS, LANES), U32),
            pltpu.VMEM((SUBLANES, d), F32),
            pltpu.SemaphoreType.DMA(()),
            pltpu.SemaphoreType.DMA((PEER_BUFS,)),
            pltpu.VMEM((2, PEER_N_KEYS, tb), F32),
            pltpu.VMEM((2, PEER_TOPK, tb), F32), pltpu.VMEM((2, PEER_TOPK, tb), F32),
            pltpu.VMEM((PAIR_ROWS, tb), F32), pltpu.VMEM((PAIR_ROWS, tb), F32),
            pltpu.VMEM((PEER_TOPK, tb), F32), pltpu.VMEM((PEER_TOPK, tb), F32),
        ],
        compiler_params=pltpu.CompilerParams(dimension_semantics=("arbitrary",), vmem_limit_bytes=PEER_VMEM_LIMIT),
        name="peer_route_mix",
    )(xn_b, xn_b, wq_heads, sub_keys[0], sub_keys[1], xn, h, pg, table)


def peer_layer(h, norm_g, w_q, sub_keys, u_tabs, v_tabs, layer, post_gains, post_outs):
    xn, xn_b = rmsnorm(h, [norm_g], [(0, F32), (0, BF16)])
    return peer_route_and_mix(xn, xn_b, w_q, sub_keys, h, pack_tables(u_tabs, v_tabs, layer), post_gains, post_outs)


def kernel(x, norm_mix, norm_ffn, norm_final, m_in_proj, m_conv_w, m_conv_b, m_dt_bias, m_a_log, m_d, m_norm,
           m_out_proj, kv_norm, w_kv, sb_w_q, sb_w_o, peer_w_q, peer_sub_keys, peer_u, peer_v):
    batch, seq, d_model = x.shape
    t = batch * seq
    h = x.reshape(t, d_model)

    n_heads = m_dt_bias.shape[1]
    proj_w = m_in_proj.shape[2] - n_heads
    u, = rmsnorm(h, [norm_mix[0]], [(0, BF16)])
    w_in = m_in_proj[0]
    zx = matmul(u, w_in, n=proj_w, name="in_proj")
    w_dt = jnp.pad(w_in[:, proj_w:], ((0, 0), (0, LANES - n_heads)))
    dtr = matmul(u, w_dt, name="in_proj_dt")
    y = mamba_ssd(zx, dtr, m_conv_w[0], m_conv_b[0], m_dt_bias[0], m_a_log[0], m_d[0], m_norm[0], batch, seq)
    h = matmul(y, m_out_proj[0], residual=h, tn=512, name="out_proj")
    h, kvn, qn = peer_layer(h, norm_ffn[0], peer_w_q[0], peer_sub_keys[0], peer_u, peer_v, 0,
                            [kv_norm, norm_mix[1]], [(0, BF16), (1, BF16)])

    kv = matmul(kvn, w_kv, out_dtype=BF16, name="kv_proj")
    q = matmul(qn, sb_w_q[0], name="q_proj")
    o = stick_breaking_attention(q, kv, batch, seq)
    h = matmul(o, sb_w_o[0], residual=h, name="o_proj")
    _, out = peer_layer(h, norm_ffn[1], peer_w_q[1], peer_sub_keys[1], peer_u, peer_v, 1,
                        [norm_final], [(0, F32)])
    return out.reshape(batch, seq, d_model)
```

```python
import functools
import math

import jax
import jax.numpy as jnp
from jax import lax
from jax.experimental import pallas as pl
from jax.experimental.pallas import tpu as pltpu

F32 = jnp.float32
BF16 = jnp.bfloat16
U32 = jnp.uint32
HIGHEST = lax.Precision.HIGHEST

NORM_EPS = 1e-6
LANES = 128
SUBLANES = 8

M_HEAD_DIM = 64
M_D_STATE = 128
M_GROUPS = 8
M_D_CONV = 4
M_CHUNK = 128
M_NORM_EPS = 1e-5

SB_HEAD_DIM = 128
SB_BLOCK = 128
SB_Q_TILE = 512
SB_HEADS_PER_STEP = 8

PEER_HEADS = 8
PEER_N_KEYS = 128
PEER_TOPK = 16
PEER_HALF = 128
PEER_SLOTS = PEER_HEADS * PEER_TOPK
PEER_BUFS = 8
PEER_AHEAD = 6
PEER_LEAD = 2
PEER_RING = PEER_LEAD + 1

VMEM_LIMIT = 48 * 1024 * 1024
PEER_VMEM_LIMIT = 56 * 1024 * 1024


def _params(*sem):
    return pltpu.CompilerParams(dimension_semantics=sem, vmem_limit_bytes=VMEM_LIMIT)


def _rmsnorm_body(x_ref, g_ref, *o_refs, eps, gain_of_out):
    x = x_ref[...]
    xn = x * lax.rsqrt(jnp.mean(x * x, axis=-1, keepdims=True) + eps)
    for o_ref, gi in zip(o_refs, gain_of_out):
        o_ref[...] = (xn * g_ref[gi:gi + 1, :]).astype(o_ref.dtype)


def _split_bf16(x):
    hi = x.astype(BF16)
    return hi, (x - hi.astype(F32)).astype(BF16)


def rmsnorm(x, gains, outs, tm=256):
    t, d = x.shape
    g = jnp.stack(gains)
    blk = pl.BlockSpec((tm, d), lambda i: (i, 0))
    return pl.pallas_call(
        functools.partial(_rmsnorm_body, eps=NORM_EPS, gain_of_out=tuple(gi for gi, _ in outs)),
        grid=(t // tm,),
        in_specs=[blk, pl.BlockSpec(g.shape, lambda i: (0, 0))],
        out_specs=[blk for _ in outs],
        out_shape=[jax.ShapeDtypeStruct((t, d), dt) for _, dt in outs],
        compiler_params=_params("arbitrary"),
        name="rmsnorm",
    )(x, g)


def _mm_body(a_ref, w_ref, *rest):
    o_ref, wb_ref = rest[-2:]

    @pl.when(pl.program_id(1) == 0)
    def _():
        wb_ref[...] = w_ref[...].astype(BF16)

    acc = jnp.dot(a_ref[...], wb_ref[...], preferred_element_type=F32)
    if len(rest) == 3:
        acc = rest[0][...] + acc
    o_ref[...] = acc.astype(o_ref.dtype)


def matmul(a, w, n=None, residual=None, out_dtype=F32, tm=1024, tn=1024, by_column_block=False, name="matmul"):
    m, k = a.shape
    n = w.shape[1] if n is None else n
    tn = min(tn, n)
    assert m % tm == 0 and n % tn == 0
    in_specs = [pl.BlockSpec((tm, k), lambda j, i: (i, 0)), pl.BlockSpec((k, tn), lambda j, i: (0, j))]
    args = [a, w]
    if residual is not None:
        in_specs.append(pl.BlockSpec((tm, tn), lambda j, i: (i, j)))
        args.append(residual)
    if by_column_block:
        out_spec = pl.BlockSpec((None, tm, tn), lambda j, i: (j, i, 0))
        out_shape = (n // tn, m, tn)
    else:
        out_spec = pl.BlockSpec((tm, tn), lambda j, i: (i, j))
        out_shape = (m, n)
    return pl.pallas_call(
        _mm_body,
        grid=(n // tn, m // tm),
        in_specs=in_specs,
        out_specs=out_spec,
        out_shape=jax.ShapeDtypeStruct(out_shape, out_dtype),
        scratch_shapes=[pltpu.VMEM((k, tn), BF16)],
        compiler_params=_params("arbitrary", "arbitrary"),
        name=name,
    )(*args)


def _softplus(x):
    return jnp.maximum(x, 0.0) + jnp.log1p(jnp.exp(-jnp.abs(x)))


def _silu(x):
    return x * jax.nn.sigmoid(x)


def _causal_conv_silu(raw_ref, prev_ref, w_ref, bias_ref, out_ref):
    cur = raw_ref[...]
    prev8 = prev_ref[...]
    w = w_ref[...]
    bias = bias_ref[...]
    top = cur[0:SUBLANES, :]
    row8 = lax.broadcasted_iota(jnp.int32, top.shape, 0)
    acc = cur * w[M_D_CONV - 1:M_D_CONV, :]
    acc_top = top * w[M_D_CONV - 1:M_D_CONV, :]
    for d in range(1, M_D_CONV):
        wd = w[M_D_CONV - 1 - d:M_D_CONV - d, :]
        shifted = pltpu.roll(cur, d, 0)
        acc = acc + shifted * wd
        shifted_top = jnp.where(row8 < d, pltpu.roll(prev8, d, 0), shifted[0:SUBLANES, :])
        acc_top = acc_top + shifted_top * wd
    out_ref[...] = _silu(acc + bias)
    out_ref[0:SUBLANES, :] = _silu(acc_top + bias)
    prev_ref[...] = cur[M_CHUNK - SUBLANES:M_CHUNK, :]


def _mamba_body(z_ref, x_ref, b_ref, c_ref, dtr_ref, cwx_ref, cwb_ref, cwc_ref, cbx_ref, cbb_ref, cbc_ref,
                dtb_ref, alog_ref, dexp_ref, nw_ref, e_ref, o_ref,
                state_ref, px_ref, pb_ref, pc_ref, xc_ref, bc_ref, cc_ref, y_ref):
    @pl.when(pl.program_id(1) == 0)
    def _():
        state_ref[...] = jnp.zeros_like(state_ref)
        px_ref[...] = jnp.zeros_like(px_ref)
        pb_ref[...] = jnp.zeros_like(pb_ref)
        pc_ref[...] = jnp.zeros_like(pc_ref)

    _causal_conv_silu(x_ref, px_ref, cwx_ref, cbx_ref, xc_ref)
    _causal_conv_silu(b_ref, pb_ref, cwb_ref, cbb_ref, bc_ref)
    _causal_conv_silu(c_ref, pc_ref, cwc_ref, cbc_ref, cc_ref)

    L = M_CHUNK
    row = lax.broadcasted_iota(jnp.int32, (L, L), 0)
    col = lax.broadcasted_iota(jnp.int32, (L, L), 1)
    causal = col <= row
    ltri = causal.astype(F32)

    dt = _softplus(dtr_ref[...] + dtb_ref[...])
    dta = dt * (-jnp.exp(alog_ref[...]))
    acs = jnp.dot(ltri, dta, preferred_element_type=F32, precision=HIGHEST)
    dt_t = dt.T
    acs_t = acs.T
    e = e_ref[...]

    def expand(v):
        hi, rest = v.astype(BF16), v
        rest = rest - hi.astype(F32)
        mid = rest.astype(BF16)
        lo = (rest - mid.astype(F32)).astype(BF16)
        return (jnp.dot(hi, e, preferred_element_type=F32) + jnp.dot(mid, e, preferred_element_type=F32)
                + jnp.dot(lo, e, preferred_element_type=F32))

    dt_exp = expand(dt)
    acs_exp = expand(acs)

    gw = M_HEAD_DIM * 8
    first_half = col < M_HEAD_DIM
    for g in range(M_GROUPS):
        bg = bc_ref[:, g * M_D_STATE:(g + 1) * M_D_STATE].astype(BF16)
        cg = cc_ref[:, g * M_D_STATE:(g + 1) * M_D_STATE].astype(BF16)
        xg = xc_ref[:, g * gw:(g + 1) * gw]
        ae = acs_exp[:, g * gw:(g + 1) * gw]
        de = dt_exp[:, g * gw:(g + 1) * gw]
        cb = lax.dot_general(cg, bg, (((1,), (1,)), ((), ())), preferred_element_type=F32)
        sg = state_ref[g]
        y_off = jnp.dot(cg, sg.astype(BF16), preferred_element_type=F32) * jnp.exp(ae)
        a_last = ae[L - 1:L, :]
        to_end = jnp.exp(a_last - ae) * de
        xw = (xg * to_end).astype(BF16)
        upd = lax.dot_general(bg, xw, (((0,), (0,)), ((), ())), preferred_element_type=F32)
        state_ref[g] = sg * jnp.exp(a_last) + upd
        for jp in range(4):
            xpair = xg[:, jp * LANES:(jp + 1) * LANES]
            ys = jnp.zeros((L, LANES), F32)
            for half in range(2):
                h = g * 8 + jp * 2 + half
                a_col = jnp.sum(jnp.where(col == h, acs, 0.0), axis=1, keepdims=True)
                a_row = acs_t[h:h + 1, :]
                d_row = dt_t[h:h + 1, :]
                decay = jnp.exp(jnp.where(causal, a_col - a_row, -jnp.inf))
                w = (cb * decay * d_row).astype(BF16)
                keep = first_half if half == 0 else jnp.logical_not(first_half)
                xm = jnp.where(keep, xpair, 0.0).astype(BF16)
                ys = ys + jnp.dot(w, xm, preferred_element_type=F32)
            lo = g * gw + jp * LANES
            y_ref[:, lo:lo + LANES] = (ys + y_off[:, jp * LANES:(jp + 1) * LANES]
                                       + xpair * dexp_ref[:, lo:lo + LANES])

    for g in range(M_GROUPS):
        sl = slice(g * gw, (g + 1) * gw)
        yf = y_ref[:, sl] * _silu(z_ref[:, sl])
        ms = jnp.mean(yf * yf, axis=1, keepdims=True)
        o_ref[:, sl] = (yf * lax.rsqrt(ms + M_NORM_EPS) * nw_ref[:, sl]).astype(o_ref.dtype)


def mamba_ssd(zx, dtr, conv_w, conv_b, dt_bias, a_log, d_skip, norm_w, batch, seq):
    t = zx.shape[0]
    d_inner = norm_w.shape[0]
    n_heads = d_inner // M_HEAD_DIM
    bc_w = M_GROUPS * M_D_STATE
    nc = seq // M_CHUNK
    assert d_inner % 2048 == 0 and bc_w == 1024 and zx.shape[1] == 2 * d_inner + 2 * bc_w

    pad_h = LANES - n_heads
    dtb = jnp.pad(dt_bias, (0, pad_h)).reshape(1, LANES)
    alog = jnp.pad(a_log, (0, pad_h)).reshape(1, LANES)
    dexp = jnp.repeat(d_skip, M_HEAD_DIM).reshape(1, d_inner)
    expand = (jnp.arange(LANES)[:, None] == (jnp.arange(d_inner)[None, :] // M_HEAD_DIM)).astype(BF16)
    cwx, cwb, cwc = conv_w[:, :d_inner], conv_w[:, d_inner:d_inner + bc_w], conv_w[:, d_inner + bc_w:]
    cb2 = conv_b.reshape(1, -1)
    cbx, cbb, cbc = cb2[:, :d_inner], cb2[:, d_inner:d_inner + bc_w], cb2[:, d_inner + bc_w:]

    row_map = lambda b, c: (b * nc + c, 0)
    const = lambda b, c: (0, 0)
    xblk = d_inner // bc_w
    in_specs = [
        pl.BlockSpec((M_CHUNK, d_inner), row_map),
        pl.BlockSpec((M_CHUNK, d_inner), lambda b, c: (b * nc + c, 1)),
        pl.BlockSpec((M_CHUNK, bc_w), lambda b, c: (b * nc + c, 2 * xblk)),
        pl.BlockSpec((M_CHUNK, bc_w), lambda b, c: (b * nc + c, 2 * xblk + 1)),
        pl.BlockSpec((M_CHUNK, LANES), row_map),
        pl.BlockSpec((M_D_CONV, d_inner), const), pl.BlockSpec((M_D_CONV, bc_w), const),
        pl.BlockSpec((M_D_CONV, bc_w), const),
        pl.BlockSpec((1, d_inner), const), pl.BlockSpec((1, bc_w), const), pl.BlockSpec((1, bc_w), const),
        pl.BlockSpec((1, LANES), const), pl.BlockSpec((1, LANES), const),
        pl.BlockSpec((1, d_inner), const), pl.BlockSpec((1, d_inner), const),
        pl.BlockSpec((LANES, d_inner), const),
    ]
    return pl.pallas_call(
        _mamba_body,
        grid=(batch, nc),
        in_specs=in_specs,
        out_specs=pl.BlockSpec((M_CHUNK, d_inner), row_map),
        out_shape=jax.ShapeDtypeStruct((t, d_inner), BF16),
        scratch_shapes=[
            pltpu.VMEM((M_GROUPS, M_D_STATE, M_HEAD_DIM * 8), F32),
            pltpu.VMEM((SUBLANES, d_inner), F32), pltpu.VMEM((SUBLANES, bc_w), F32),
            pltpu.VMEM((SUBLANES, bc_w), F32),
            pltpu.VMEM((M_CHUNK, d_inner), F32), pltpu.VMEM((M_CHUNK, bc_w), F32),
            pltpu.VMEM((M_CHUNK, bc_w), F32),
            pltpu.VMEM((M_CHUNK, d_inner), F32),
        ],
        compiler_params=_params("arbitrary", "arbitrary"),
        name="mamba_ssd",
    )(zx, zx, zx, zx, dtr, cwx, cwb, cwc, cbx, cbb, cbc, dtb, alog, dexp, norm_w.reshape(1, d_inner), expand)


def _attn_body(q_ref, k_ref, v_ref, o_ref, *scratch, scale):
    acc_refs = scratch[:SB_HEADS_PER_STEP]
    tail_refs = scratch[SB_HEADS_PER_STEP:]
    qt = pl.program_id(2)
    tq, blk, hd = SB_Q_TILE, SB_BLOCK, SB_HEAD_DIM
    per_tile = tq // blk
    row = lax.broadcasted_iota(jnp.int32, (tq, blk), 0)
    col = lax.broadcasted_iota(jnp.int32, (tq, blk), 1)
    r2 = lax.broadcasted_iota(jnp.int32, (blk, 2 * blk), 0)
    c2 = lax.broadcasted_iota(jnp.int32, (blk, 2 * blk), 1)
    sums = ((r2 > c2) | (c2 >= blk)).astype(BF16)
    for ref in scratch:
        ref[...] = jnp.zeros_like(ref)

    def make_step(first_kb, masked):
        def step(i, carry):
            kb = first_kb - i
            start = pl.multiple_of(kb * blk, blk)
            r0 = (per_tile - 1 - i) * blk if masked else 0
            rows = slice(r0, tq)
            if masked:
                key_pos = kb * blk + lax.broadcasted_iota(jnp.int32, (tq - r0, blk), 1)
                strict = key_pos < qt * tq + r0 + lax.broadcasted_iota(jnp.int32, (tq - r0, blk), 0)
            heads = range(SB_HEADS_PER_STEP)
            lanes = [slice(h * hd, (h + 1) * hd) for h in heads]
            z = [lax.dot_general((q_ref[rows, lanes[h]] * scale).astype(BF16), k_ref[pl.ds(start, blk), lanes[h]],
                                 (((1,), (1,)), ((), ())), preferred_element_type=F32) for h in heads]
            log_1mb, log_b, s = [], [], []
            for h in heads:
                lp = jnp.log(1.0 + jnp.exp(-jnp.abs(z[h])))
                l1 = -(jnp.maximum(z[h], 0.0) + lp)
                log_b.append(z[h] + l1)
                log_1mb.append(jnp.where(strict, l1, 0.0) if masked else l1)
            for h in heads:
                hi, lo = _split_bf16(log_1mb[h])
                s.append(jnp.dot(hi, sums, preferred_element_type=F32)
                         + jnp.dot(lo, sums, preferred_element_type=F32))
            for h in heads:
                tail = tail_refs[h][rows, :]
                a = jnp.exp(log_b[h] + s[h][:, :blk] + tail)
                if masked:
                    a = jnp.where(strict, a, 0.0)
                acc_refs[h][rows, :] += jnp.dot(a.astype(BF16), v_ref[pl.ds(start, blk), lanes[h]],
                                                preferred_element_type=F32)
                tail_refs[h][rows, :] = tail + s[h][:, blk:]
            return carry
        return step

    diagonal = make_step(qt * per_tile + per_tile - 1, True)
    for i in range(per_tile):
        diagonal(i, 0)
    lax.fori_loop(0, qt * per_tile, make_step(qt * per_tile - 1, False), 0)
    for h in range(SB_HEADS_PER_STEP):
        o_ref[:, h * hd:(h + 1) * hd] = acc_refs[h][...].astype(o_ref.dtype)


def stick_breaking_attention(q, kv, batch, seq):
    t, width = q.shape
    nq = seq // SB_Q_TILE
    gw = SB_HEADS_PER_STEP * SB_HEAD_DIM
    n_groups = width // gw
    return pl.pallas_call(
        functools.partial(_attn_body, scale=1.0 / math.sqrt(SB_HEAD_DIM)),
        grid=(batch, n_groups, nq),
        in_specs=[
            pl.BlockSpec((SB_Q_TILE, gw), lambda b, g, i: (b * nq + i, g)),
            pl.BlockSpec((seq, gw), lambda b, g, i: (b, g)),
            pl.BlockSpec((seq, gw), lambda b, g, i: (b, n_groups + g)),
        ],
        out_specs=pl.BlockSpec((SB_Q_TILE, gw), lambda b, g, i: (b * nq + i, g)),
        out_shape=jax.ShapeDtypeStruct((t, width), BF16),
        scratch_shapes=([pltpu.VMEM((SB_Q_TILE, SB_HEAD_DIM), F32)] * SB_HEADS_PER_STEP
                        + [pltpu.VMEM((SB_Q_TILE, SB_BLOCK), F32)] * SB_HEADS_PER_STEP),
        compiler_params=_params("arbitrary", "arbitrary", "arbitrary"),
        name="stick_breaking_attention",
    )(q, kv, kv)


def _take_best(s_ref, ids, v_ref, i_ref, e):
    s = s_ref[...]
    m = jnp.max(s, axis=0, keepdims=True)
    first = jnp.min(jnp.where(s == m, ids, jnp.inf), axis=0, keepdims=True)
    s_ref[...] = jnp.where(ids == first, -jnp.inf, s)
    v_ref[e:e + 1, :] = m
    i_ref[e:e + 1, :] = first


def _pair_candidates(v1, v2):
    k = PEER_TOPK
    tokens = v1.shape[1]
    sub = lax.broadcasted_iota(jnp.int32, (SUBLANES, tokens), 0)
    sub16 = lax.broadcasted_iota(jnp.int32, (k, tokens), 0)
    vals, ids = [], []
    vals.append(v1[0:1, :] + v2)
    ids.append(sub16)
    vals.append(jnp.where(sub16 >= 1, v1 + v2[0:1, :], -jnp.inf))
    ids.append(sub16 * k)
    for a in range(1, 5):
        b_max = k // (a + 1) - 1
        vals.append(jnp.where((sub >= 1) & (sub <= b_max), v1[a:a + 1, :] + v2[0:SUBLANES, :], -jnp.inf))
        ids.append(sub + a * k)
    vals.append(jnp.where((sub >= 5) & (sub <= 7), v1[0:SUBLANES, :] + v2[1:2, :], -jnp.inf))
    ids.append(sub * k + 1)
    return jnp.concatenate(vals, axis=0), jnp.concatenate(ids, axis=0).astype(F32)


PAIR_ROWS = 2 * PEER_TOPK + 5 * SUBLANES
ROUTE_KINDS = 4
ROUTE_PIECES = ROUTE_KINDS * PEER_HEADS


def _route_step(kind, e, head, slot, q_ref, k1_ref, k2_ref, idx_t_ref, gate_t_ref,
                s_ref, v_ref, i_ref, cand_ref, cid_ref, sc_ref, ci_ref):
    if kind == 0:
        if e == 0:
            nt = (((1,), (1,)), ((), ()))
            q = q_ref[head]
            s_ref[0] = lax.dot_general(k1_ref[...], q[:, :PEER_HALF], nt, preferred_element_type=F32,
                                       precision=HIGHEST)
            s_ref[1] = lax.dot_general(k2_ref[...], q[:, PEER_HALF:], nt, preferred_element_type=F32,
                                       precision=HIGHEST)
        return
    if kind in (1, 2):
        half = kind - 1
        key_ids = lax.broadcasted_iota(jnp.int32, s_ref.shape[1:], 0).astype(F32)
        _take_best(s_ref.at[half], key_ids, v_ref.at[half], i_ref.at[half], e)
        return
    if e == 0:
        cand_ref[...], cid_ref[...] = _pair_candidates(v_ref[0], v_ref[1])
    _take_best(cand_ref, cid_ref[...], sc_ref, ci_ref, e)
    if e == PEER_TOPK - 1:
        sc = sc_ref[...]
        ci = ci_ref[...].astype(jnp.int32)
        i1 = i_ref[0].astype(jnp.int32)
        i2 = i_ref[1].astype(jnp.int32)
        ca = ci >> 4
        cbi = ci & (PEER_TOPK - 1)
        e1 = jnp.zeros_like(ci)
        e2 = jnp.zeros_like(ci)
        for a in range(PEER_TOPK):
            e1 = jnp.where(ca == a, i1[a:a + 1, :], e1)
            e2 = jnp.where(cbi == a, i2[a:a + 1, :], e2)
        p = jnp.exp(sc - jnp.max(sc, axis=0, keepdims=True))
        gate = p / jnp.sum(p, axis=0, keepdims=True)
        rows = pl.ds(pl.multiple_of(head * PEER_TOPK, PEER_TOPK), PEER_TOPK)
        idx_t_ref[slot, rows, :] = e1 * PEER_N_KEYS + e2
        gate_t_ref[slot, rows, :] = gate


def _pack_pairs(x):
    half = x.shape[1] // 2
    bits = lax.bitcast_convert_type(x.astype(BF16).astype(F32), U32)
    return (bits[:, :half] >> 16) | (bits[:, half:] & jnp.uint32(0xFFFF0000))


def _unpack_pairs(w):
    lo = lax.bitcast_convert_type(w << 16, F32)
    hi = lax.bitcast_convert_type(w & jnp.uint32(0xFFFF0000), F32)
    return lo, hi


def _pack_tables_body(u_ref, v_ref, o_ref):
    half_chunks = o_ref.shape[1] // 2
    for t, tab_ref in enumerate((u_ref, v_ref)):
        packed = _pack_pairs(tab_ref[...])
        for c in range(half_chunks):
            o_ref[:, t * half_chunks + c, 0, :] = packed[:, c * LANES:(c + 1) * LANES]


def pack_tables(u_tabs, v_tabs, layer, tm=256):
    _, n, d = u_tabs.shape
    n_chunks = d // LANES
    layer_rows = pl.BlockSpec((None, tm, d), lambda i: (layer, i, 0))
    return pl.pallas_call(
        _pack_tables_body,
        grid=(n // tm,),
        in_specs=[layer_rows, layer_rows],
        out_specs=pl.BlockSpec((tm, n_chunks, 1, LANES), lambda i: (i, 0, 0, 0)),
        out_shape=jax.ShapeDtypeStruct((n, n_chunks, 1, LANES), U32),
        compiler_params=_params("arbitrary"),
        name="peer_pack_tables",
    )(u_tabs, v_tabs)


def _gelu(x):
    return 0.5 * x * (1.0 + lax.erf(x * (1.0 / math.sqrt(2.0))))


def _peer_body(xb0_ref, xbn_ref, wq_ref, k1_ref, k2_ref, x_ref, h_ref, pg_ref, tab_hbm, o_ref, *rest,
               tb, n_blocks, post_gain_of_out):
    n_post = len(post_gain_of_out)
    post_refs = rest[:n_post]
    (q_ring, idx_t_ref, gate_t_ref, idx_rows, gate_rows, idx_smem, buf, mix_ref, idx_sem,
     row_sem) = rest[n_post:n_post + 10]
    route_scratch = rest[n_post + 10:]
    step = pl.program_id(0)
    blk = step - PEER_LEAD
    mixing = blk >= 0
    n_groups = tb // SUBLANES
    route_slot = lax.rem(step, PEER_RING)
    q_cur = q_ring.at[lax.rem(step, 2)]
    q_next = q_ring.at[lax.rem(step + 1, 2)]
    kc_w = wq_ref.shape[1] // (2 * ROUTE_KINDS)

    @pl.when(step == 0)
    def _():
        for hd in range(PEER_HEADS):
            q_cur[hd] = jnp.dot(xb0_ref[...], wq_ref[hd], preferred_element_type=F32)

    def project(kind, head, u):
        kc = 2 * kind + u
        ks = slice(kc * kc_w, (kc + 1) * kc_w)
        part = jnp.dot(xbn_ref[:, ks], wq_ref[head, ks, :], preferred_element_type=F32)
        if kc == 0:
            q_next[head] = part
        else:
            q_next[head] += part

    def route_step(kind, e, head):
        _route_step(kind, e, head, route_slot, q_cur, k1_ref, k2_ref, idx_t_ref, gate_t_ref, *route_scratch)

    def route_piece(kind, head):
        for u in range(2):
            project(kind, head, u)
        for e in range(PEER_TOPK):
            route_step(kind, e, head)

    def row_copy(j, slot, k):
        e = idx_smem[j, k]
        return pltpu.make_async_copy(tab_hbm.at[e], buf.at[slot, :, pl.ds(k, 1), :], row_sem.at[slot])

    def wait(slot):
        pltpu.make_async_copy(buf.at[slot], buf.at[slot], row_sem.at[slot]).wait()

    @pl.when(mixing)
    def _():
        slot = lax.rem(blk, PEER_RING)
        nxt = lax.rem(blk + 1, PEER_RING)
        for j in range(tb // LANES):
            cols = slice(j * LANES, (j + 1) * LANES)
            gate_rows[cols, :] = gate_t_ref[slot, :, cols].T
            idx_rows[cols, :] = idx_t_ref[slot, :, cols].astype(F32).T.astype(jnp.int32)
        head_rows = idx_t_ref[nxt, :, 0:LANES].astype(F32).T.astype(jnp.int32)
        idx_rows[tb:tb + PEER_BUFS, :] = head_rows[0:PEER_BUFS, :]
        ids = pltpu.make_async_copy(idx_rows, idx_smem, idx_sem)
        ids.start()
        ids.wait()

    @pl.when(blk == 0)
    def _():
        for j in range(PEER_AHEAD):
            def body(k, carry):
                row_copy(j, j, k).start()
                return carry
            lax.fori_loop(0, PEER_SLOTS, body, 0, unroll=8)

    d = x_ref.shape[1]
    half_chunks = d // (2 * LANES)
    row = lax.broadcasted_iota(jnp.int32, (PEER_SLOTS, LANES), 0)
    col = lax.broadcasted_iota(jnp.int32, (PEER_SLOTS, LANES), 1)
    eye = row == col
    per_chunk = PEER_SLOTS // (2 * half_chunks)

    def mix_group(g, kind):
        base = pl.multiple_of(g * SUBLANES, SUBLANES)
        rows = pl.ds(base, SUBLANES)
        gate8 = gate_rows[rows, :]
        head = g // ROUTE_KINDS
        steps_per_token = PEER_TOPK // SUBLANES

        for r in range(SUBLANES):
            ahead_slot = (r + PEER_AHEAD) % PEER_BUFS
            started = [0]

            def start_some():
                for k in range(started[0], started[0] + per_chunk):
                    row_copy(base + r + PEER_AHEAD, ahead_slot, k).start(priority=k % 2)
                started[0] += per_chunk

            wait(r)
            acc = jnp.zeros((PEER_SLOTS, LANES), F32)
            for c in range(half_chunks):
                lo, hi = _unpack_pairs(buf[r, c])
                x_lo = x_ref[rows, c * LANES:(c + 1) * LANES][r:r + 1, :]
                x_hi = x_ref[rows, (half_chunks + c) * LANES:(half_chunks + c + 1) * LANES][r:r + 1, :]
                acc = acc + lo * x_lo + hi * x_hi
                start_some()
                if c == 0 and r in (2, 5):
                    project(kind, head, r // 3)
                if c == half_chunks // 2:
                    route_step(kind, steps_per_token * r, head)
            hid = jnp.sum(acc, axis=1, keepdims=True)
            gate_row = jnp.broadcast_to(gate8[r:r + 1, :], (PEER_SLOTS, LANES))
            gate = jnp.sum(jnp.where(eye, gate_row, 0.0), axis=1, keepdims=True)
            w = jnp.broadcast_to(gate * _gelu(hid), (PEER_SLOTS, LANES))
            for c in range(half_chunks):
                lo, hi = _unpack_pairs(buf[r, half_chunks + c])
                mix_ref[r:r + 1, c * LANES:(c + 1) * LANES] = jnp.sum(lo * w, axis=0, keepdims=True)
                mix_ref[r:r + 1, (half_chunks + c) * LANES:(half_chunks + c + 1) * LANES] = (
                    jnp.sum(hi * w, axis=0, keepdims=True))
                start_some()
                if c == half_chunks // 2:
                    for e in range(steps_per_token * r + 1, steps_per_token * (r + 1)):
                        route_step(kind, e, head)
        y = h_ref[rows, :] + mix_ref[...]
        o_ref[rows, :] = y
        if post_refs:
            yn = y * lax.rsqrt(jnp.mean(y * y, axis=-1, keepdims=True) + NORM_EPS)
            for p_ref, gi in zip(post_refs, post_gain_of_out):
                p_ref[rows, :] = (yn * pg_ref[gi:gi + 1, :]).astype(p_ref.dtype)

    def group(g, carry):
        kind = lax.rem(g, ROUTE_KINDS)

        @pl.when(mixing)
        def _():
            lax.switch(kind, [functools.partial(mix_group, kind=k) for k in range(ROUTE_KINDS)], g)

        @pl.when(jnp.logical_not(mixing))
        def _():
            lax.switch(kind, [functools.partial(route_piece, k) for k in range(ROUTE_KINDS)], g // ROUTE_KINDS)
        return carry

    lax.fori_loop(0, n_groups, group, 0)

    @pl.when(blk == n_blocks - 1)
    def _():
        for j in range(PEER_AHEAD):
            wait(j)


def peer_route_and_mix(xn, xn_b, w_q, sub_keys, h, table, post_gains, post_outs, tb=256):
    t, d = xn.shape
    n_blocks = t // tb
    n_chunks = d // LANES
    qw = 2 * PEER_HALF
    assert PEER_BUFS == SUBLANES and PEER_AHEAD < PEER_BUFS and PEER_SLOTS % n_chunks == 0
    assert tb % LANES == 0 and tb // SUBLANES == ROUTE_PIECES and n_blocks * tb == t
    assert table.shape[1:] == (n_chunks, 1, LANES) and w_q.shape == (d, PEER_HEADS * qw)
    wq_heads = w_q.astype(BF16).reshape(d, PEER_HEADS, qw).transpose(1, 0, 2)
    pg = jnp.stack(post_gains)
    mixed = lambda s: (jnp.maximum(s - PEER_LEAD, 0), 0)
    keys = pl.BlockSpec((PEER_N_KEYS, PEER_HALF), lambda s: (0, 0))
    row_blk = pl.BlockSpec((tb, d), mixed)
    return pl.pallas_call(
        functools.partial(_peer_body, tb=tb, n_blocks=n_blocks, post_gain_of_out=tuple(gi for gi, _ in post_outs)),
        grid=(n_blocks + PEER_LEAD,),
        in_specs=[
            pl.BlockSpec((tb, d), lambda s: (0, 0)),
            pl.BlockSpec((tb, d), lambda s: (jnp.minimum(s + 1, n_blocks - 1), 0)),
            pl.BlockSpec((PEER_HEADS, d, qw), lambda s: (0, 0, 0)),
            keys, keys,
            row_blk,
            row_blk,
            pl.BlockSpec(pg.shape, lambda s: (0, 0)),
            pl.BlockSpec(memory_space=pl.ANY),
        ],
        out_specs=[row_blk] + [row_blk for _ in post_outs],
        out_shape=[jax.ShapeDtypeStruct((t, d), F32)] + [jax.ShapeDtypeStruct((t, d), dt) for _, dt in post_outs],
        scratch_shapes=[
            pltpu.VMEM((2, PEER_HEADS, tb, qw), F32),
            pltpu.VMEM((PEER_RING, PEER_SLOTS, tb), jnp.int32),
            pltpu.VMEM((PEER_RING, PEER_SLOTS, tb), F32),
            pltpu.VMEM((tb + PEER_BUFS, PEER_SLOTS), jnp.int32),
            pltpu.VMEM((tb, PEER_SLOTS), F32),
            pltpu.SMEM((tb + PEER_BUFS, PEER_SLOTS), jnp.int32),
            pltpu.VMEM((PEER_BUFS, n_chunks, PEER_SLOTS, LANES), U32),
            pltpu.VMEM((SUBLANES, d), F32),
            pltpu.SemaphoreType.DMA(()),
            pltpu.SemaphoreType.DMA((PEER_BUFS,)),
            pltpu.VMEM((2, PEER_N_KEYS, tb), F32),
            pltpu.VMEM((2, PEER_TOPK, tb), F32), pltpu.VMEM((2, PEER_TOPK, tb), F32),
            pltpu.VMEM((PAIR_ROWS, tb), F32), pltpu.VMEM((PAIR_ROWS, tb), F32),
            pltpu.VMEM((PEER_TOPK, tb), F32), pltpu.VMEM((PEER_TOPK, tb), F32),
        ],
        compiler_params=pltpu.CompilerParams(dimension_semantics=("arbitrary",), vmem_limit_bytes=PEER_VMEM_LIMIT),
        name="peer_route_mix",
    )(xn_b, xn_b, wq_heads, sub_keys[0], sub_keys[1], xn, h, pg, table)


def peer_layer(h, norm_g, w_q, sub_keys, u_tabs, v_tabs, layer, post_gains, post_outs):
    xn, xn_b = rmsnorm(h, [norm_g], [(0, F32), (0, BF16)])
    return peer_route_and_mix(xn, xn_b, w_q, sub_keys, h, pack_tables(u_tabs, v_tabs, layer), post_gains, post_outs)


def kernel(x, norm_mix, norm_ffn, norm_final, m_in_proj, m_conv_w, m_conv_b, m_dt_bias, m_a_log, m_d, m_norm,
           m_out_proj, kv_norm, w_kv, sb_w_q, sb_w_o, peer_w_q, peer_sub_keys, peer_u, peer_v):
    batch, seq, d_model = x.shape
    t = batch * seq
    h = x.reshape(t, d_model)

    n_heads = m_dt_bias.shape[1]
    proj_w = m_in_proj.shape[2] - n_heads
    u, = rmsnorm(h, [norm_mix[0]], [(0, BF16)])
    w_in = m_in_proj[0]
    zx = matmul(u, w_in, n=proj_w, name="in_proj")
    w_dt = jnp.pad(w_in[:, proj_w:], ((0, 0), (0, LANES - n_heads)))
    dtr = matmul(u, w_dt, name="in_proj_dt")
    y = mamba_ssd(zx, dtr, m_conv_w[0], m_conv_b[0], m_dt_bias[0], m_a_log[0], m_d[0], m_norm[0], batch, seq)
    h = matmul(y, m_out_proj[0], residual=h, tn=512, name="out_proj")
    h, kvn, qn = peer_layer(h, norm_ffn[0], peer_w_q[0], peer_sub_keys[0], peer_u, peer_v, 0,
                            [kv_norm, norm_mix[1]], [(0, BF16), (1, BF16)])

    kv = matmul(kvn, w_kv, out_dtype=BF16, name="kv_proj")
    q = matmul(qn, sb_w_q[0], name="q_proj")
    o = stick_breaking_attention(q, kv, batch, seq)
    h = matmul(o, sb_w_o[0], residual=h, name="o_proj")
    _, out = peer_layer(h, norm_ffn[1], peer_w_q[1], peer_sub_keys[1], peer_u, peer_v, 1,
                        [norm_final], [(0, F32)])
    return out.reshape(batch, seq, d_model)
```

```python
import functools
import math

import jax
import jax.numpy as jnp
from jax import lax
from jax.experimental import pallas as pl
from jax.experimental.pallas import tpu as pltpu

F32 = jnp.float32
BF16 = jnp.bfloat16
U32 = jnp.uint32
HIGHEST = lax.Precision.HIGHEST

NORM_EPS = 1e-6
LANES = 128
SUBLANES = 8

M_HEAD_DIM = 64
M_D_STATE = 128
M_GROUPS = 8
M_D_CONV = 4
M_CHUNK = 128
M_NORM_EPS = 1e-5

SB_HEAD_DIM = 128
SB_BLOCK = 128
SB_Q_TILE = 512
SB_HEADS_PER_STEP = 8

PEER_HEADS = 8
PEER_N_KEYS = 128
PEER_TOPK = 16
PEER_HALF = 128
PEER_SLOTS = PEER_HEADS * PEER_TOPK
PEER_BUFS = 8
PEER_AHEAD = 6
PEER_LEAD = 2
PEER_RING = PEER_LEAD + 1

VMEM_LIMIT = 48 * 1024 * 1024
PEER_VMEM_LIMIT = 56 * 1024 * 1024


def _params(*sem):
    return pltpu.CompilerParams(dimension_semantics=sem, vmem_limit_bytes=VMEM_LIMIT)


def _rmsnorm_body(x_ref, g_ref, *o_refs, eps, gain_of_out):
    x = x_ref[...]
    xn = x * lax.rsqrt(jnp.mean(x * x, axis=-1, keepdims=True) + eps)
    for o_ref, gi in zip(o_refs, gain_of_out):
        o_ref[...] = (xn * g_ref[gi:gi + 1, :]).astype(o_ref.dtype)


def _split_bf16(x):
    hi = x.astype(BF16)
    return hi, (x - hi.astype(F32)).astype(BF16)


def rmsnorm(x, gains, outs, tm=256):
    t, d = x.shape
    g = jnp.stack(gains)
    blk = pl.BlockSpec((tm, d), lambda i: (i, 0))
    return pl.pallas_call(
        functools.partial(_rmsnorm_body, eps=NORM_EPS, gain_of_out=tuple(gi for gi, _ in outs)),
        grid=(t // tm,),
        in_specs=[blk, pl.BlockSpec(g.shape, lambda i: (0, 0))],
        out_specs=[blk for _ in outs],
        out_shape=[jax.ShapeDtypeStruct((t, d), dt) for _, dt in outs],
        compiler_params=_params("arbitrary"),
        name="rmsnorm",
    )(x, g)


def _mm_body(a_ref, w_ref, *rest):
    o_ref, wb_ref = rest[-2:]

    @pl.when(pl.program_id(1) == 0)
    def _():
        wb_ref[...] = w_ref[...].astype(BF16)

    acc = jnp.dot(a_ref[...], wb_ref[...], preferred_element_type=F32)
    if len(rest) == 3:
        acc = rest[0][...] + acc
    o_ref[...] = acc.astype(o_ref.dtype)


def matmul(a, w, n=None, residual=None, out_dtype=F32, tm=1024, tn=1024, by_column_block=False, name="matmul"):
    m, k = a.shape
    n = w.shape[1] if n is None else n
    tn = min(tn, n)
    assert m % tm == 0 and n % tn == 0
    in_specs = [pl.BlockSpec((tm, k), lambda j, i: (i, 0)), pl.BlockSpec((k, tn), lambda j, i: (0, j))]
    args = [a, w]
    if residual is not None:
        in_specs.append(pl.BlockSpec((tm, tn), lambda j, i: (i, j)))
        args.append(residual)
    if by_column_block:
        out_spec = pl.BlockSpec((None, tm, tn), lambda j, i: (j, i, 0))
        out_shape = (n // tn, m, tn)
    else:
        out_spec = pl.BlockSpec((tm, tn), lambda j, i: (i, j))
        out_shape = (m, n)
    return pl.pallas_call(
        _mm_body,
        grid=(n // tn, m // tm),
        in_specs=in_specs,
        out_specs=out_spec,
        out_shape=jax.ShapeDtypeStruct(out_shape, out_dtype),
        scratch_shapes=[pltpu.VMEM((k, tn), BF16)],
        compiler_params=_params("arbitrary", "arbitrary"),
        name=name,
    )(*args)


def _softplus(x):
    return jnp.maximum(x, 0.0) + jnp.log1p(jnp.exp(-jnp.abs(x)))


def _silu(x):
    return x * jax.nn.sigmoid(x)


def _causal_conv_silu(raw_ref, prev_ref, w_ref, bias_ref, out_ref):
    cur = raw_ref[...]
    prev8 = prev_ref[...]
    w = w_ref[...]
    bias = bias_ref[...]
    top = cur[0:SUBLANES, :]
    row8 = lax.broadcasted_iota(jnp.int32, top.shape, 0)
    acc = cur * w[M_D_CONV - 1:M_D_CONV, :]
    acc_top = top * w[M_D_CONV - 1:M_D_CONV, :]
    for d in range(1, M_D_CONV):
        wd = w[M_D_CONV - 1 - d:M_D_CONV - d, :]
        shifted = pltpu.roll(cur, d, 0)
        acc = acc + shifted * wd
        shifted_top = jnp.where(row8 < d, pltpu.roll(prev8, d, 0), shifted[0:SUBLANES, :])
        acc_top = acc_top + shifted_top * wd
    out_ref[...] = _silu(acc + bias)
    out_ref[0:SUBLANES, :] = _silu(acc_top + bias)
    prev_ref[...] = cur[M_CHUNK - SUBLANES:M_CHUNK, :]


def _mamba_body(z_ref, x_ref, b_ref, c_ref, dtr_ref, cwx_ref, cwb_ref, cwc_ref, cbx_ref, cbb_ref, cbc_ref,
                dtb_ref, alog_ref, dexp_ref, nw_ref, e_ref, o_ref,
                state_ref, px_ref, pb_ref, pc_ref, xc_ref, bc_ref, cc_ref, y_ref):
    @pl.when(pl.program_id(1) == 0)
    def _():
        state_ref[...] = jnp.zeros_like(state_ref)
        px_ref[...] = jnp.zeros_like(px_ref)
        pb_ref[...] = jnp.zeros_like(pb_ref)
        pc_ref[...] = jnp.zeros_like(pc_ref)

    _causal_conv_silu(x_ref, px_ref, cwx_ref, cbx_ref, xc_ref)
    _causal_conv_silu(b_ref, pb_ref, cwb_ref, cbb_ref, bc_ref)
    _causal_conv_silu(c_ref, pc_ref, cwc_ref, cbc_ref, cc_ref)

    L = M_CHUNK
    row = lax.broadcasted_iota(jnp.int32, (L, L), 0)
    col = lax.broadcasted_iota(jnp.int32, (L, L), 1)
    causal = col <= row
    ltri = causal.astype(F32)

    dt = _softplus(dtr_ref[...] + dtb_ref[...])
    dta = dt * (-jnp.exp(alog_ref[...]))
    acs = jnp.dot(ltri, dta, preferred_element_type=F32, precision=HIGHEST)
    dt_t = dt.T
    acs_t = acs.T
    e = e_ref[...]

    def expand(v):
        hi, rest = v.astype(BF16), v
        rest = rest - hi.astype(F32)
        mid = rest.astype(BF16)
        lo = (rest - mid.astype(F32)).astype(BF16)
        return (jnp.dot(hi, e, preferred_element_type=F32) + jnp.dot(mid, e, preferred_element_type=F32)
                + jnp.dot(lo, e, preferred_element_type=F32))

    dt_exp = expand(dt)
    acs_exp = expand(acs)

    gw = M_HEAD_DIM * 8
    first_half = col < M_HEAD_DIM
    for g in range(M_GROUPS):
        bg = bc_ref[:, g * M_D_STATE:(g + 1) * M_D_STATE].astype(BF16)
        cg = cc_ref[:, g * M_D_STATE:(g + 1) * M_D_STATE].astype(BF16)
        xg = xc_ref[:, g * gw:(g + 1) * gw]
        ae = acs_exp[:, g * gw:(g + 1) * gw]
        de = dt_exp[:, g * gw:(g + 1) * gw]
        cb = lax.dot_general(cg, bg, (((1,), (1,)), ((), ())), preferred_element_type=F32)
        sg = state_ref[g]
        y_off = jnp.dot(cg, sg.astype(BF16), preferred_element_type=F32) * jnp.exp(ae)
        a_last = ae[L - 1:L, :]
        to_end = jnp.exp(a_last - ae) * de
        xw = (xg * to_end).astype(BF16)
        upd = lax.dot_general(bg, xw, (((0,), (0,)), ((), ())), preferred_element_type=F32)
        state_ref[g] = sg * jnp.exp(a_last) + upd
        for jp in range(4):
            xpair = xg[:, jp * LANES:(jp + 1) * LANES]
            ys = jnp.zeros((L, LANES), F32)
            for half in range(2):
                h = g * 8 + jp * 2 + half
                a_col = jnp.sum(jnp.where(col == h, acs, 0.0), axis=1, keepdims=True)
                a_row = acs_t[h:h + 1, :]
                d_row = dt_t[h:h + 1, :]
                decay = jnp.exp(jnp.where(causal, a_col - a_row, -jnp.inf))
                w = (cb * decay * d_row).astype(BF16)
                keep = first_half if half == 0 else jnp.logical_not(first_half)
                xm = jnp.where(keep, xpair, 0.0).astype(BF16)
                ys = ys + jnp.dot(w, xm, preferred_element_type=F32)
            lo = g * gw + jp * LANES
            y_ref[:, lo:lo + LANES] = (ys + y_off[:, jp * LANES:(jp + 1) * LANES]
                                       + xpair * dexp_ref[:, lo:lo + LANES])

    for g in range(M_GROUPS):
        sl = slice(g * gw, (g + 1) * gw)
        yf = y_ref[:, sl] * _silu(z_ref[:, sl])
        ms = jnp.mean(yf * yf, axis=1, keepdims=True)
        o_ref[:, sl] = (yf * lax.rsqrt(ms + M_NORM_EPS) * nw_ref[:, sl]).astype(o_ref.dtype)


def mamba_ssd(zx, dtr, conv_w, conv_b, dt_bias, a_log, d_skip, norm_w, batch, seq):
    t = zx.shape[0]
    d_inner = norm_w.shape[0]
    n_heads = d_inner // M_HEAD_DIM
    bc_w = M_GROUPS * M_D_STATE
    nc = seq // M_CHUNK
    assert d_inner % 2048 == 0 and bc_w == 1024 and zx.shape[1] == 2 * d_inner + 2 * bc_w

    pad_h = LANES - n_heads
    dtb = jnp.pad(dt_bias, (0, pad_h)).reshape(1, LANES)
    alog = jnp.pad(a_log, (0, pad_h)).reshape(1, LANES)
    dexp = jnp.repeat(d_skip, M_HEAD_DIM).reshape(1, d_inner)
    expand = (jnp.arange(LANES)[:, None] == (jnp.arange(d_inner)[None, :] // M_HEAD_DIM)).astype(BF16)
    cwx, cwb, cwc = conv_w[:, :d_inner], conv_w[:, d_inner:d_inner + bc_w], conv_w[:, d_inner + bc_w:]
    cb2 = conv_b.reshape(1, -1)
    cbx, cbb, cbc = cb2[:, :d_inner], cb2[:, d_inner:d_inner + bc_w], cb2[:, d_inner + bc_w:]

    row_map = lambda b, c: (b * nc + c, 0)
    const = lambda b, c: (0, 0)
    xblk = d_inner // bc_w
    in_specs = [
        pl.BlockSpec((M_CHUNK, d_inner), row_map),
        pl.BlockSpec((M_CHUNK, d_inner), lambda b, c: (b * nc + c, 1)),
        pl.BlockSpec((M_CHUNK, bc_w), lambda b, c: (b * nc + c, 2 * xblk)),
        pl.BlockSpec((M_CHUNK, bc_w), lambda b, c: (b * nc + c, 2 * xblk + 1)),
        pl.BlockSpec((M_CHUNK, LANES), row_map),
        pl.BlockSpec((M_D_CONV, d_inner), const), pl.BlockSpec((M_D_CONV, bc_w), const),
        pl.BlockSpec((M_D_CONV, bc_w), const),
        pl.BlockSpec((1, d_inner), const), pl.BlockSpec((1, bc_w), const), pl.BlockSpec((1, bc_w), const),
        pl.BlockSpec((1, LANES), const), pl.BlockSpec((1, LANES), const),
        pl.BlockSpec((1, d_inner), const), pl.BlockSpec((1, d_inner), const),
        pl.BlockSpec((LANES, d_inner), const),
    ]
    return pl.pallas_call(
        _mamba_body,
        grid=(batch, nc),
        in_specs=in_specs,
        out_specs=pl.BlockSpec((M_CHUNK, d_inner), row_map),
        out_shape=jax.ShapeDtypeStruct((t, d_inner), BF16),
        scratch_shapes=[
            pltpu.VMEM((M_GROUPS, M_D_STATE, M_HEAD_DIM * 8), F32),
            pltpu.VMEM((SUBLANES, d_inner), F32), pltpu.VMEM((SUBLANES, bc_w), F32),
            pltpu.VMEM((SUBLANES, bc_w), F32),
            pltpu.VMEM((M_CHUNK, d_inner), F32), pltpu.VMEM((M_CHUNK, bc_w), F32),
            pltpu.VMEM((M_CHUNK, bc_w), F32),
            pltpu.VMEM((M_CHUNK, d_inner), F32),
        ],
        compiler_params=_params("arbitrary", "arbitrary"),
        name="mamba_ssd",
    )(zx, zx, zx, zx, dtr, cwx, cwb, cwc, cbx, cbb, cbc, dtb, alog, dexp, norm_w.reshape(1, d_inner), expand)


def _attn_body(q_ref, k_ref, v_ref, o_ref, *scratch, scale):
    acc_refs = scratch[:SB_HEADS_PER_STEP]
    tail_refs = scratch[SB_HEADS_PER_STEP:]
    qt = pl.program_id(2)
    tq, blk, hd = SB_Q_TILE, SB_BLOCK, SB_HEAD_DIM
    per_tile = tq // blk
    row = lax.broadcasted_iota(jnp.int32, (tq, blk), 0)
    col = lax.broadcasted_iota(jnp.int32, (tq, blk), 1)
    r2 = lax.broadcasted_iota(jnp.int32, (blk, 2 * blk), 0)
    c2 = lax.broadcasted_iota(jnp.int32, (blk, 2 * blk), 1)
    sums = ((r2 > c2) | (c2 >= blk)).astype(BF16)
    sums2 = jnp.concatenate([sums, sums], axis=0)
    for ref in scratch:
        ref[...] = jnp.zeros_like(ref)

    def make_step(first_kb, masked):
        def step(i, carry):
            kb = first_kb - i
            start = pl.multiple_of(kb * blk, blk)
            r0 = (per_tile - 1 - i) * blk if masked else 0
            rows = slice(r0, tq)
            if masked:
                key_pos = kb * blk + lax.broadcasted_iota(jnp.int32, (tq - r0, blk), 1)
                strict = key_pos < qt * tq + r0 + lax.broadcasted_iota(jnp.int32, (tq - r0, blk), 0)
            heads = range(SB_HEADS_PER_STEP)
            lanes = [slice(h * hd, (h + 1) * hd) for h in heads]
            z = [lax.dot_general((q_ref[rows, lanes[h]] * scale).astype(BF16), k_ref[pl.ds(start, blk), lanes[h]],
                                 (((1,), (1,)), ((), ())), preferred_element_type=F32) for h in heads]
            log_1mb, log_b, s = [], [], []
            for h in heads:
                lp = jnp.log(1.0 + jnp.exp(-jnp.abs(z[h])))
                l1 = -(jnp.maximum(z[h], 0.0) + lp)
                log_b.append(z[h] + l1)
                log_1mb.append(jnp.where(strict, l1, 0.0) if masked else l1)
            for h in heads:
                hi, lo = _split_bf16(log_1mb[h])
                s.append(jnp.dot(jnp.concatenate([hi, lo], axis=1), sums2,
                                 preferred_element_type=F32))
            for h in heads:
                tail = tail_refs[h][rows, :]
                a = jnp.exp(log_b[h] + s[h][:, :blk] + tail)
                if masked:
                    a = jnp.where(strict, a, 0.0)
                acc_refs[h][rows, :] += jnp.dot(a.astype(BF16), v_ref[pl.ds(start, blk), lanes[h]],
                                                preferred_element_type=F32)
                tail_refs[h][rows, :] = tail + s[h][:, blk:]
            return carry
        return step

    diagonal = make_step(qt * per_tile + per_tile - 1, True)
    for i in range(per_tile):
        diagonal(i, 0)
    lax.fori_loop(0, qt * per_tile, make_step(qt * per_tile - 1, False), 0)
    for h in range(SB_HEADS_PER_STEP):
        o_ref[:, h * hd:(h + 1) * hd] = acc_refs[h][...].astype(o_ref.dtype)


def stick_breaking_attention(q, kv, batch, seq):
    t, width = q.shape
    nq = seq // SB_Q_TILE
    gw = SB_HEADS_PER_STEP * SB_HEAD_DIM
    n_groups = width // gw
    return pl.pallas_call(
        functools.partial(_attn_body, scale=1.0 / math.sqrt(SB_HEAD_DIM)),
        grid=(batch, n_groups, nq),
        in_specs=[
            pl.BlockSpec((SB_Q_TILE, gw), lambda b, g, i: (b * nq + i, g)),
            pl.BlockSpec((seq, gw), lambda b, g, i: (b, g)),
            pl.BlockSpec((seq, gw), lambda b, g, i: (b, n_groups + g)),
        ],
        out_specs=pl.BlockSpec((SB_Q_TILE, gw), lambda b, g, i: (b * nq + i, g)),
        out_shape=jax.ShapeDtypeStruct((t, width), BF16),
        scratch_shapes=([pltpu.VMEM((SB_Q_TILE, SB_HEAD_DIM), F32)] * SB_HEADS_PER_STEP
                        + [pltpu.VMEM((SB_Q_TILE, SB_BLOCK), F32)] * SB_HEADS_PER_STEP),
        compiler_params=_params("arbitrary", "arbitrary", "arbitrary"),
        name="stick_breaking_attention",
    )(q, kv, kv)


def _take_best(s_ref, ids, v_ref, i_ref, e):
    s = s_ref[...]
    m = jnp.max(s, axis=0, keepdims=True)
    first = jnp.min(jnp.where(s == m, ids, jnp.inf), axis=0, keepdims=True)
    s_ref[...] = jnp.where(ids == first, -jnp.inf, s)
    v_ref[e:e + 1, :] = m
    i_ref[e:e + 1, :] = first


def _pair_candidates(v1, v2):
    k = PEER_TOPK
    tokens = v1.shape[1]
    sub = lax.broadcasted_iota(jnp.int32, (SUBLANES, tokens), 0)
    sub16 = lax.broadcasted_iota(jnp.int32, (k, tokens), 0)
    vals, ids = [], []
    vals.append(v1[0:1, :] + v2)
    ids.append(sub16)
    vals.append(jnp.where(sub16 >= 1, v1 + v2[0:1, :], -jnp.inf))
    ids.append(sub16 * k)
    for a in range(1, 5):
        b_max = k // (a + 1) - 1
        vals.append(jnp.where((sub >= 1) & (sub <= b_max), v1[a:a + 1, :] + v2[0:SUBLANES, :], -jnp.inf))
        ids.append(sub + a * k)
    vals.append(jnp.where((sub >= 5) & (sub <= 7), v1[0:SUBLANES, :] + v2[1:2, :], -jnp.inf))
    ids.append(sub * k + 1)
    return jnp.concatenate(vals, axis=0), jnp.concatenate(ids, axis=0).astype(F32)


PAIR_ROWS = 2 * PEER_TOPK + 5 * SUBLANES
ROUTE_KINDS = 4
ROUTE_PIECES = ROUTE_KINDS * PEER_HEADS


def _route_step(kind, e, head, slot, q_ref, k1_ref, k2_ref, idx_t_ref, gate_t_ref,
                s_ref, v_ref, i_ref, cand_ref, cid_ref, sc_ref, ci_ref):
    if kind == 0:
        if e == 0:
            nt = (((1,), (1,)), ((), ()))
            q = q_ref[head]
            s_ref[0] = lax.dot_general(k1_ref[...], q[:, :PEER_HALF], nt, preferred_element_type=F32,
                                       precision=HIGHEST)
            s_ref[1] = lax.dot_general(k2_ref[...], q[:, PEER_HALF:], nt, preferred_element_type=F32,
                                       precision=HIGHEST)
        return
    if kind in (1, 2):
        half = kind - 1
        key_ids = lax.broadcasted_iota(jnp.int32, s_ref.shape[1:], 0).astype(F32)
        _take_best(s_ref.at[half], key_ids, v_ref.at[half], i_ref.at[half], e)
        return
    if e == 0:
        cand_ref[...], cid_ref[...] = _pair_candidates(v_ref[0], v_ref[1])
    _take_best(cand_ref, cid_ref[...], sc_ref, ci_ref, e)
    if e == PEER_TOPK - 1:
        sc = sc_ref[...]
        ci = ci_ref[...].astype(jnp.int32)
        i1 = i_ref[0].astype(jnp.int32)
        i2 = i_ref[1].astype(jnp.int32)
        ca = ci >> 4
        cbi = ci & (PEER_TOPK - 1)
        e1 = jnp.zeros_like(ci)
        e2 = jnp.zeros_like(ci)
        for a in range(PEER_TOPK):
            e1 = jnp.where(ca == a, i1[a:a + 1, :], e1)
            e2 = jnp.where(cbi == a, i2[a:a + 1, :], e2)
        p = jnp.exp(sc - jnp.max(sc, axis=0, keepdims=True))
        gate = p / jnp.sum(p, axis=0, keepdims=True)
        rows = pl.ds(pl.multiple_of(head * PEER_TOPK, PEER_TOPK), PEER_TOPK)
        idx_t_ref[slot, rows, :] = e1 * PEER_N_KEYS + e2
        gate_t_ref[slot, rows, :] = gate


def _pack_pairs(x):
    half = x.shape[1] // 2
    bits = lax.bitcast_convert_type(x.astype(BF16).astype(F32), U32)
    return (bits[:, :half] >> 16) | (bits[:, half:] & jnp.uint32(0xFFFF0000))


def _unpack_pairs(w):
    lo = lax.bitcast_convert_type(w << 16, F32)
    hi = lax.bitcast_convert_type(w & jnp.uint32(0xFFFF0000), F32)
    return lo, hi


def _pack_tables_body(u_ref, v_ref, o_ref):
    half_chunks = o_ref.shape[1] // 2
    for t, tab_ref in enumerate((u_ref, v_ref)):
        packed = _pack_pairs(tab_ref[...])
        for c in range(half_chunks):
            o_ref[:, t * half_chunks + c, 0, :] = packed[:, c * LANES:(c + 1) * LANES]


def pack_tables(u_tabs, v_tabs, layer, tm=256):
    _, n, d = u_tabs.shape
    n_chunks = d // LANES
    layer_rows = pl.BlockSpec((None, tm, d), lambda i: (layer, i, 0))
    return pl.pallas_call(
        _pack_tables_body,
        grid=(n // tm,),
        in_specs=[layer_rows, layer_rows],
        out_specs=pl.BlockSpec((tm, n_chunks, 1, LANES), lambda i: (i, 0, 0, 0)),
        out_shape=jax.ShapeDtypeStruct((n, n_chunks, 1, LANES), U32),
        compiler_params=_params("arbitrary"),
        name="peer_pack_tables",
    )(u_tabs, v_tabs)


def _gelu(x):
    return 0.5 * x * (1.0 + lax.erf(x * (1.0 / math.sqrt(2.0))))


def _peer_body(xb0_ref, xbn_ref, wq_ref, k1_ref, k2_ref, x_ref, h_ref, pg_ref, tab_hbm, o_ref, *rest,
               tb, n_blocks, post_gain_of_out):
    n_post = len(post_gain_of_out)
    post_refs = rest[:n_post]
    (q_ring, idx_t_ref, gate_t_ref, idx_rows, gate_rows, idx_smem, buf, mix_ref, idx_sem,
     row_sem) = rest[n_post:n_post + 10]
    route_scratch = rest[n_post + 10:]
    step = pl.program_id(0)
    blk = step - PEER_LEAD
    mixing = blk >= 0
    n_groups = tb // SUBLANES
    route_slot = lax.rem(step, PEER_RING)
    q_cur = q_ring.at[lax.rem(step, 2)]
    q_next = q_ring.at[lax.rem(step + 1, 2)]
    kc_w = wq_ref.shape[1] // (2 * ROUTE_KINDS)

    @pl.when(step == 0)
    def _():
        for hd in range(PEER_HEADS):
            q_cur[hd] = jnp.dot(xb0_ref[...], wq_ref[hd], preferred_element_type=F32)

    def project(kind, head, u):
        kc = 2 * kind + u
        ks = slice(kc * kc_w, (kc + 1) * kc_w)
        part = jnp.dot(xbn_ref[:, ks], wq_ref[head, ks, :], preferred_element_type=F32)
        if kc == 0:
            q_next[head] = part
        else:
            q_next[head] += part

    def route_step(kind, e, head):
        _route_step(kind, e, head, route_slot, q_cur, k1_ref, k2_ref, idx_t_ref, gate_t_ref, *route_scratch)

    def route_piece(kind, head):
        for u in range(2):
            project(kind, head, u)
        for e in range(PEER_TOPK):
            route_step(kind, e, head)

    def row_copy(j, slot, k):
        e = idx_smem[j, k]
        return pltpu.make_async_copy(tab_hbm.at[e], buf.at[slot, :, pl.ds(k, 1), :], row_sem.at[slot])

    def wait(slot):
        pltpu.make_async_copy(buf.at[slot], buf.at[slot], row_sem.at[slot]).wait()

    @pl.when(mixing)
    def _():
        slot = lax.rem(blk, PEER_RING)
        nxt = lax.rem(blk + 1, PEER_RING)
        for j in range(tb // LANES):
            cols = slice(j * LANES, (j + 1) * LANES)
            gate_rows[cols, :] = gate_t_ref[slot, :, cols].T
            idx_rows[cols, :] = idx_t_ref[slot, :, cols].astype(F32).T.astype(jnp.int32)
        head_rows = idx_t_ref[nxt, :, 0:LANES].astype(F32).T.astype(jnp.int32)
        idx_rows[tb:tb + PEER_BUFS, :] = head_rows[0:PEER_BUFS, :]
        ids = pltpu.make_async_copy(idx_rows, idx_smem, idx_sem)
        ids.start()
        ids.wait()

    @pl.when(blk == 0)
    def _():
        for j in range(PEER_AHEAD):
            def body(k, carry):
                row_copy(j, j, k).start()
                return carry
            lax.fori_loop(0, PEER_SLOTS, body, 0, unroll=8)

    d = x_ref.shape[1]
    half_chunks = d // (2 * LANES)
    row = lax.broadcasted_iota(jnp.int32, (PEER_SLOTS, LANES), 0)
    col = lax.broadcasted_iota(jnp.int32, (PEER_SLOTS, LANES), 1)
    eye = row == col
    per_chunk = PEER_SLOTS // (2 * half_chunks)

    def mix_group(g, kind):
        base = pl.multiple_of(g * SUBLANES, SUBLANES)
        rows = pl.ds(base, SUBLANES)
        gate8 = gate_rows[rows, :]
        head = g // ROUTE_KINDS
        steps_per_token = PEER_TOPK // SUBLANES

        for r in range(SUBLANES):
            ahead_slot = (r + PEER_AHEAD) % PEER_BUFS
            started = [0]

            def start_some():
                for k in range(started[0], started[0] + per_chunk):
                    row_copy(base + r + PEER_AHEAD, ahead_slot, k).start(priority=k % 2)
                started[0] += per_chunk

            wait(r)
            acc = jnp.zeros((PEER_SLOTS, LANES), F32)
            for c in range(half_chunks):
                lo, hi = _unpack_pairs(buf[r, c])
                x_lo = x_ref[rows, c * LANES:(c + 1) * LANES][r:r + 1, :]
                x_hi = x_ref[rows, (half_chunks + c) * LANES:(half_chunks + c + 1) * LANES][r:r + 1, :]
                acc = acc + lo * x_lo + hi * x_hi
                start_some()
                if c == 0 and r in (2, 5):
                    project(kind, head, r // 3)
                if c == half_chunks // 2:
                    route_step(kind, steps_per_token * r, head)
            hid = jnp.sum(acc, axis=1, keepdims=True)
            gate_row = jnp.broadcast_to(gate8[r:r + 1, :], (PEER_SLOTS, LANES))
            gate = jnp.sum(jnp.where(eye, gate_row, 0.0), axis=1, keepdims=True)
            w = jnp.broadcast_to(gate * _gelu(hid), (PEER_SLOTS, LANES))
            for c in range(half_chunks):
                lo, hi = _unpack_pairs(buf[r, half_chunks + c])
                mix_ref[r:r + 1, c * LANES:(c + 1) * LANES] = jnp.sum(lo * w, axis=0, keepdims=True)
                mix_ref[r:r + 1, (half_chunks + c) * LANES:(half_chunks + c + 1) * LANES] = (
                    jnp.sum(hi * w, axis=0, keepdims=True))
                start_some()
                if c == half_chunks // 2:
                    for e in range(steps_per_token * r + 1, steps_per_token * (r + 1)):
                        route_step(kind, e, head)
        y = h_ref[rows, :] + mix_ref[...]
        o_ref[rows, :] = y
        if post_refs:
            yn = y * lax.rsqrt(jnp.mean(y * y, axis=-1, keepdims=True) + NORM_EPS)
            for p_ref, gi in zip(post_refs, post_gain_of_out):
                p_ref[rows, :] = (yn * pg_ref[gi:gi + 1, :]).astype(p_ref.dtype)

    def group(g, carry):
        kind = lax.rem(g, ROUTE_KINDS)

        @pl.when(mixing)
        def _():
            lax.switch(kind, [functools.partial(mix_group, kind=k) for k in range(ROUTE_KINDS)], g)

        @pl.when(jnp.logical_not(mixing))
        def _():
            lax.switch(kind, [functools.partial(route_piece, k) for k in range(ROUTE_KINDS)], g // ROUTE_KINDS)
        return carry

    lax.fori_loop(0, n_groups, group, 0)

    @pl.when(blk == n_blocks - 1)
    def _():
        for j in range(PEER_AHEAD):
            wait(j)


def peer_route_and_mix(xn, xn_b, w_q, sub_keys, h, table, post_gains, post_outs, tb=256):
    t, d = xn.shape
    n_blocks = t // tb
    n_chunks = d // LANES
    qw = 2 * PEER_HALF
    assert PEER_BUFS == SUBLANES and PEER_AHEAD < PEER_BUFS and PEER_SLOTS % n_chunks == 0
    assert tb % LANES == 0 and tb // SUBLANES == ROUTE_PIECES and n_blocks * tb == t
    assert table.shape[1:] == (n_chunks, 1, LANES) and w_q.shape == (d, PEER_HEADS * qw)
    wq_heads = w_q.astype(BF16).reshape(d, PEER_HEADS, qw).transpose(1, 0, 2)
    pg = jnp.stack(post_gains)
    mixed = lambda s: (jnp.maximum(s - PEER_LEAD, 0), 0)
    keys = pl.BlockSpec((PEER_N_KEYS, PEER_HALF), lambda s: (0, 0))
    row_blk = pl.BlockSpec((tb, d), mixed)
    return pl.pallas_call(
        functools.partial(_peer_body, tb=tb, n_blocks=n_blocks, post_gain_of_out=tuple(gi for gi, _ in post_outs)),
        grid=(n_blocks + PEER_LEAD,),
        in_specs=[
            pl.BlockSpec((tb, d), lambda s: (0, 0)),
            pl.BlockSpec((tb, d), lambda s: (jnp.minimum(s + 1, n_blocks - 1), 0)),
            pl.BlockSpec((PEER_HEADS, d, qw), lambda s: (0, 0, 0)),
            keys, keys,
            row_blk,
            row_blk,
            pl.BlockSpec(pg.shape, lambda s: (0, 0)),
            pl.BlockSpec(memory_space=pl.ANY),
        ],
        out_specs=[row_blk] + [row_blk for _ in post_outs],
        out_shape=[jax.ShapeDtypeStruct((t, d), F32)] + [jax.ShapeDtypeStruct((t, d), dt) for _, dt in post_outs],
        scratch_shapes=[
            pltpu.VMEM((2, PEER_HEADS, tb, qw), F32),
            pltpu.VMEM((PEER_RING, PEER_SLOTS, tb), jnp.int32),
            pltpu.VMEM((PEER_RING, PEER_SLOTS, tb), F32),
            pltpu.VMEM((tb + PEER_BUFS, PEER_SLOTS), jnp.int32),
            pltpu.VMEM((tb, PEER_SLOTS), F32),
            pltpu.SMEM((tb + PEER_BUFS, PEER_SLOTS), jnp.int32),
            pltpu.VMEM((PEER_BUFS, n_chunks, PEER_SLOTS, LANES), U32),
            pltpu.VMEM((SUBLANES, d), F32),
            pltpu.SemaphoreType.DMA(()),
            pltpu.SemaphoreType.DMA((PEER_BUFS,)),
            pltpu.VMEM((2, PEER_N_KEYS, tb), F32),
            pltpu.VMEM((2, PEER_TOPK, tb), F32), pltpu.VMEM((2, PEER_TOPK, tb), F32),
            pltpu.VMEM((PAIR_ROWS, tb), F32), pltpu.VMEM((PAIR_ROWS, tb), F32),
            pltpu.VMEM((PEER_TOPK, tb), F32), pltpu.VMEM((PEER_TOPK, tb), F32),
        ],
        compiler_params=pltpu.CompilerParams(dimension_semantics=("arbitrary",), vmem_limit_bytes=PEER_VMEM_LIMIT),
        name="peer_route_mix",
    )(xn_b, xn_b, wq_heads, sub_keys[0], sub_keys[1], xn, h, pg, table)


def peer_layer(h, norm_g, w_q, sub_keys, u_tabs, v_tabs, layer, post_gains, post_outs):
    xn, xn_b = rmsnorm(h, [norm_g], [(0, F32), (0, BF16)])
    return peer_route_and_mix(xn, xn_b, w_q, sub_keys, h, pack_tables(u_tabs, v_tabs, layer), post_gains, post_outs)


def kernel(x, norm_mix, norm_ffn, norm_final, m_in_proj, m_conv_w, m_conv_b, m_dt_bias, m_a_log, m_d, m_norm,
           m_out_proj, kv_norm, w_kv, sb_w_q, sb_w_o, peer_w_q, peer_sub_keys, peer_u, peer_v):
    batch, seq, d_model = x.shape
    t = batch * seq
    h = x.reshape(t, d_model)

    n_heads = m_dt_bias.shape[1]
    proj_w = m_in_proj.shape[2] - n_heads
    u, = rmsnorm(h, [norm_mix[0]], [(0, BF16)])
    w_in = m_in_proj[0]
    zx = matmul(u, w_in, n=proj_w, name="in_proj")
    w_dt = jnp.pad(w_in[:, proj_w:], ((0, 0), (0, LANES - n_heads)))
    dtr = matmul(u, w_dt, name="in_proj_dt")
    y = mamba_ssd(zx, dtr, m_conv_w[0], m_conv_b[0], m_dt_bias[0], m_a_log[0], m_d[0], m_norm[0], batch, seq)
    h = matmul(y, m_out_proj[0], residual=h, tn=512, name="out_proj")
    h, kvn, qn = peer_layer(h, norm_ffn[0], peer_w_q[0], peer_sub_keys[0], peer_u, peer_v, 0,
                            [kv_norm, norm_mix[1]], [(0, BF16), (1, BF16)])

    kv = matmul(kvn, w_kv, out_dtype=BF16, name="kv_proj")
    q = matmul(qn, sb_w_q[0], name="q_proj")
    o = stick_breaking_attention(q, kv, batch, seq)
    h = matmul(o, sb_w_o[0], residual=h, name="o_proj")
    _, out = peer_layer(h, norm_ffn[1], peer_w_q[1], peer_sub_keys[1], peer_u, peer_v, 1,
                        [norm_final], [(0, F32)])
    return out.reshape(batch, seq, d_model)
```

```python
import functools
import math

import jax
import jax.numpy as jnp
from jax import lax
from jax.experimental import pallas as pl
from jax.experimental.pallas import tpu as pltpu

F32 = jnp.float32
BF16 = jnp.bfloat16
U32 = jnp.uint32
HIGHEST = lax.Precision.HIGHEST

NORM_EPS = 1e-6
LANES = 128
SUBLANES = 8

M_HEAD_DIM = 64
M_D_STATE = 128
M_GROUPS = 8
M_D_CONV = 4
M_CHUNK = 128
M_NORM_EPS = 1e-5

SB_HEAD_DIM = 128
SB_BLOCK = 128
SB_Q_TILE = 512
SB_HEADS_PER_STEP = 8

PEER_HEADS = 8
PEER_N_KEYS = 128
PEER_TOPK = 16
PEER_HALF = 128
PEER_SLOTS = PEER_HEADS * PEER_TOPK
PEER_BUFS = 8
PEER_AHEAD = 6
PEER_LEAD = 2
PEER_RING = PEER_LEAD + 1

VMEM_LIMIT = 48 * 1024 * 1024
PEER_VMEM_LIMIT = 56 * 1024 * 1024


def _params(*sem):
    return pltpu.CompilerParams(dimension_semantics=sem, vmem_limit_bytes=VMEM_LIMIT)


def _rmsnorm_body(x_ref, g_ref, *o_refs, eps, gain_of_out):
    x = x_ref[...]
    xn = x * lax.rsqrt(jnp.mean(x * x, axis=-1, keepdims=True) + eps)
    for o_ref, gi in zip(o_refs, gain_of_out):
        o_ref[...] = (xn * g_ref[gi:gi + 1, :]).astype(o_ref.dtype)


def _split_bf16(x):
    hi = x.astype(BF16)
    return hi, (x - hi.astype(F32)).astype(BF16)


def rmsnorm(x, gains, outs, tm=256):
    t, d = x.shape
    g = jnp.stack(gains)
    blk = pl.BlockSpec((tm, d), lambda i: (i, 0))
    return pl.pallas_call(
        functools.partial(_rmsnorm_body, eps=NORM_EPS, gain_of_out=tuple(gi for gi, _ in outs)),
        grid=(t // tm,),
        in_specs=[blk, pl.BlockSpec(g.shape, lambda i: (0, 0))],
        out_specs=[blk for _ in outs],
        out_shape=[jax.ShapeDtypeStruct((t, d), dt) for _, dt in outs],
        compiler_params=_params("arbitrary"),
        name="rmsnorm",
    )(x, g)


def _mm_body(a_ref, w_ref, *rest):
    o_ref, wb_ref = rest[-2:]

    @pl.when(pl.program_id(1) == 0)
    def _():
        wb_ref[...] = w_ref[...].astype(BF16)

    acc = jnp.dot(a_ref[...], wb_ref[...], preferred_element_type=F32)
    if len(rest) == 3:
        acc = rest[0][...] + acc
    o_ref[...] = acc.astype(o_ref.dtype)


def matmul(a, w, n=None, residual=None, out_dtype=F32, tm=1024, tn=1024, name="matmul"):
    m, k = a.shape
    n = w.shape[1] if n is None else n
    tn = min(tn, n)
    assert m % tm == 0 and n % tn == 0
    in_specs = [pl.BlockSpec((tm, k), lambda j, i: (i, 0)), pl.BlockSpec((k, tn), lambda j, i: (0, j))]
    args = [a, w]
    if residual is not None:
        in_specs.append(pl.BlockSpec((tm, tn), lambda j, i: (i, j)))
        args.append(residual)
    return pl.pallas_call(
        _mm_body,
        grid=(n // tn, m // tm),
        in_specs=in_specs,
        out_specs=pl.BlockSpec((tm, tn), lambda j, i: (i, j)),
        out_shape=jax.ShapeDtypeStruct((m, n), out_dtype),
        scratch_shapes=[pltpu.VMEM((k, tn), BF16)],
        compiler_params=_params("arbitrary", "arbitrary"),
        name=name,
    )(*args)


def _softplus(x):
    return jnp.maximum(x, 0.0) + jnp.log1p(jnp.exp(-jnp.abs(x)))


def _silu(x):
    return x * jax.nn.sigmoid(x)


def _causal_conv_silu(raw_ref, prev_ref, w_ref, bias_ref, out_ref):
    cur = raw_ref[...]
    prev8 = prev_ref[...]
    w = w_ref[...]
    bias = bias_ref[...]
    top = cur[0:SUBLANES, :]
    row8 = lax.broadcasted_iota(jnp.int32, top.shape, 0)
    acc = cur * w[M_D_CONV - 1:M_D_CONV, :]
    acc_top = top * w[M_D_CONV - 1:M_D_CONV, :]
    for d in range(1, M_D_CONV):
        wd = w[M_D_CONV - 1 - d:M_D_CONV - d, :]
        shifted = pltpu.roll(cur, d, 0)
        acc = acc + shifted * wd
        shifted_top = jnp.where(row8 < d, pltpu.roll(prev8, d, 0), shifted[0:SUBLANES, :])
        acc_top = acc_top + shifted_top * wd
    out_ref[...] = _silu(acc + bias)
    out_ref[0:SUBLANES, :] = _silu(acc_top + bias)
    prev_ref[...] = cur[M_CHUNK - SUBLANES:M_CHUNK, :]


def _mamba_body(z_ref, x_ref, b_ref, c_ref, dtr_ref, cwx_ref, cwb_ref, cwc_ref, cbx_ref, cbb_ref, cbc_ref,
                dtb_ref, alog_ref, dexp_ref, nw_ref, e_ref, o_ref,
                state_ref, px_ref, pb_ref, pc_ref, xc_ref, bc_ref, cc_ref, y_ref):
    @pl.when(pl.program_id(1) == 0)
    def _():
        state_ref[...] = jnp.zeros_like(state_ref)
        px_ref[...] = jnp.zeros_like(px_ref)
        pb_ref[...] = jnp.zeros_like(pb_ref)
        pc_ref[...] = jnp.zeros_like(pc_ref)

    _causal_conv_silu(x_ref, px_ref, cwx_ref, cbx_ref, xc_ref)
    _causal_conv_silu(b_ref, pb_ref, cwb_ref, cbb_ref, bc_ref)
    _causal_conv_silu(c_ref, pc_ref, cwc_ref, cbc_ref, cc_ref)

    L = M_CHUNK
    row = lax.broadcasted_iota(jnp.int32, (L, L), 0)
    col = lax.broadcasted_iota(jnp.int32, (L, L), 1)
    causal = col <= row
    ltri = causal.astype(F32)

    dt = _softplus(dtr_ref[...] + dtb_ref[...])
    dta = dt * (-jnp.exp(alog_ref[...]))
    acs = jnp.dot(ltri, dta, preferred_element_type=F32, precision=HIGHEST)
    dt_t = dt.T
    acs_t = acs.T
    e = e_ref[...]

    def expand(v):
        hi, rest = v.astype(BF16), v
        rest = rest - hi.astype(F32)
        mid = rest.astype(BF16)
        lo = (rest - mid.astype(F32)).astype(BF16)
        return (jnp.dot(hi, e, preferred_element_type=F32) + jnp.dot(mid, e, preferred_element_type=F32)
                + jnp.dot(lo, e, preferred_element_type=F32))

    dt_exp = expand(dt)
    acs_exp = expand(acs)

    gw = M_HEAD_DIM * 8
    first_half = col < M_HEAD_DIM
    for g in range(M_GROUPS):
        bg = bc_ref[:, g * M_D_STATE:(g + 1) * M_D_STATE].astype(BF16)
        cg = cc_ref[:, g * M_D_STATE:(g + 1) * M_D_STATE].astype(BF16)
        xg = xc_ref[:, g * gw:(g + 1) * gw]
        ae = acs_exp[:, g * gw:(g + 1) * gw]
        de = dt_exp[:, g * gw:(g + 1) * gw]
        cb = lax.dot_general(cg, bg, (((1,), (1,)), ((), ())), preferred_element_type=F32)
        sg = state_ref[g]
        y_off = jnp.dot(cg, sg.astype(BF16), preferred_element_type=F32) * jnp.exp(ae)
        a_last = ae[L - 1:L, :]
        to_end = jnp.exp(a_last - ae) * de
        xw = (xg * to_end).astype(BF16)
        upd = lax.dot_general(bg, xw, (((0,), (0,)), ((), ())), preferred_element_type=F32)
        state_ref[g] = sg * jnp.exp(a_last) + upd
        for jp in range(4):
            xpair = xg[:, jp * LANES:(jp + 1) * LANES]
            ys = jnp.zeros((L, LANES), F32)
            for half in range(2):
                h = g * 8 + jp * 2 + half
                a_col = jnp.sum(jnp.where(col == h, acs, 0.0), axis=1, keepdims=True)
                a_row = acs_t[h:h + 1, :]
                d_row = dt_t[h:h + 1, :]
                decay = jnp.exp(jnp.where(causal, a_col - a_row, -jnp.inf))
                w = (cb * decay * d_row).astype(BF16)
                keep = first_half if half == 0 else jnp.logical_not(first_half)
                xm = jnp.where(keep, xpair, 0.0).astype(BF16)
                ys = ys + jnp.dot(w, xm, preferred_element_type=F32)
            lo = g * gw + jp * LANES
            y_ref[:, lo:lo + LANES] = (ys + y_off[:, jp * LANES:(jp + 1) * LANES]
                                       + xpair * dexp_ref[:, lo:lo + LANES])

    for g in range(M_GROUPS):
        sl = slice(g * gw, (g + 1) * gw)
        yf = y_ref[:, sl] * _silu(z_ref[:, sl])
        ms = jnp.mean(yf * yf, axis=1, keepdims=True)
        o_ref[:, sl] = (yf * lax.rsqrt(ms + M_NORM_EPS) * nw_ref[:, sl]).astype(o_ref.dtype)


def mamba_ssd(zx, dtr, conv_w, conv_b, dt_bias, a_log, d_skip, norm_w, batch, seq):
    t = zx.shape[0]
    d_inner = norm_w.shape[0]
    n_heads = d_inner // M_HEAD_DIM
    bc_w = M_GROUPS * M_D_STATE
    nc = seq // M_CHUNK
    assert d_inner % 2048 == 0 and bc_w == 1024 and zx.shape[1] == 2 * d_inner + 2 * bc_w

    pad_h = LANES - n_heads
    dtb = jnp.pad(dt_bias, (0, pad_h)).reshape(1, LANES)
    alog = jnp.pad(a_log, (0, pad_h)).reshape(1, LANES)
    dexp = jnp.repeat(d_skip, M_HEAD_DIM).reshape(1, d_inner)
    expand = (jnp.arange(LANES)[:, None] == (jnp.arange(d_inner)[None, :] // M_HEAD_DIM)).astype(BF16)
    cwx, cwb, cwc = conv_w[:, :d_inner], conv_w[:, d_inner:d_inner + bc_w], conv_w[:, d_inner + bc_w:]
    cb2 = conv_b.reshape(1, -1)
    cbx, cbb, cbc = cb2[:, :d_inner], cb2[:, d_inner:d_inner + bc_w], cb2[:, d_inner + bc_w:]

    row_map = lambda b, c: (b * nc + c, 0)
    const = lambda b, c: (0, 0)
    xblk = d_inner // bc_w
    in_specs = [
        pl.BlockSpec((M_CHUNK, d_inner), row_map),
        pl.BlockSpec((M_CHUNK, d_inner), lambda b, c: (b * nc + c, 1)),
        pl.BlockSpec((M_CHUNK, bc_w), lambda b, c: (b * nc + c, 2 * xblk)),
        pl.BlockSpec((M_CHUNK, bc_w), lambda b, c: (b * nc + c, 2 * xblk + 1)),
        pl.BlockSpec((M_CHUNK, LANES), row_map),
        pl.BlockSpec((M_D_CONV, d_inner), const), pl.BlockSpec((M_D_CONV, bc_w), const),
        pl.BlockSpec((M_D_CONV, bc_w), const),
        pl.BlockSpec((1, d_inner), const), pl.BlockSpec((1, bc_w), const), pl.BlockSpec((1, bc_w), const),
        pl.BlockSpec((1, LANES), const), pl.BlockSpec((1, LANES), const),
        pl.BlockSpec((1, d_inner), const), pl.BlockSpec((1, d_inner), const),
        pl.BlockSpec((LANES, d_inner), const),
    ]
    return pl.pallas_call(
        _mamba_body,
        grid=(batch, nc),
        in_specs=in_specs,
        out_specs=pl.BlockSpec((M_CHUNK, d_inner), row_map),
        out_shape=jax.ShapeDtypeStruct((t, d_inner), BF16),
        scratch_shapes=[
            pltpu.VMEM((M_GROUPS, M_D_STATE, M_HEAD_DIM * 8), F32),
            pltpu.VMEM((SUBLANES, d_inner), F32), pltpu.VMEM((SUBLANES, bc_w), F32),
            pltpu.VMEM((SUBLANES, bc_w), F32),
            pltpu.VMEM((M_CHUNK, d_inner), F32), pltpu.VMEM((M_CHUNK, bc_w), F32),
            pltpu.VMEM((M_CHUNK, bc_w), F32),
            pltpu.VMEM((M_CHUNK, d_inner), F32),
        ],
        compiler_params=_params("arbitrary", "arbitrary"),
        name="mamba_ssd",
    )(zx, zx, zx, zx, dtr, cwx, cwb, cwc, cbx, cbb, cbc, dtb, alog, dexp, norm_w.reshape(1, d_inner), expand)


def _attn_body(q_ref, k_ref, v_ref, o_ref, *scratch, scale):
    acc_refs = scratch[:SB_HEADS_PER_STEP]
    tail_refs = scratch[SB_HEADS_PER_STEP:]
    qt = pl.program_id(2)
    tq, blk, hd = SB_Q_TILE, SB_BLOCK, SB_HEAD_DIM
    per_tile = tq // blk
    row = lax.broadcasted_iota(jnp.int32, (tq, blk), 0)
    col = lax.broadcasted_iota(jnp.int32, (tq, blk), 1)
    r2 = lax.broadcasted_iota(jnp.int32, (blk, 2 * blk), 0)
    c2 = lax.broadcasted_iota(jnp.int32, (blk, 2 * blk), 1)
    sums = ((r2 > c2) | (c2 >= blk)).astype(BF16)
    sums2 = jnp.concatenate([sums, sums], axis=0)
    for ref in scratch:
        ref[...] = jnp.zeros_like(ref)

    def make_step(first_kb, masked):
        def step(i, carry):
            kb = first_kb - i
            start = pl.multiple_of(kb * blk, blk)
            r0 = (per_tile - 1 - i) * blk if masked else 0
            rows = slice(r0, tq)
            if masked:
                key_pos = kb * blk + lax.broadcasted_iota(jnp.int32, (tq - r0, blk), 1)
                strict = key_pos < qt * tq + r0 + lax.broadcasted_iota(jnp.int32, (tq - r0, blk), 0)
            heads = range(SB_HEADS_PER_STEP)
            lanes = [slice(h * hd, (h + 1) * hd) for h in heads]
            z = [lax.dot_general((q_ref[rows, lanes[h]] * scale).astype(BF16), k_ref[pl.ds(start, blk), lanes[h]],
                                 (((1,), (1,)), ((), ())), preferred_element_type=F32) for h in heads]
            log_1mb, log_b, s = [], [], []
            for h in heads:
                lp = jnp.log(1.0 + jnp.exp(-jnp.abs(z[h])))
                l1 = -(jnp.maximum(z[h], 0.0) + lp)
                log_b.append(z[h] + l1)
                log_1mb.append(jnp.where(strict, l1, 0.0) if masked else l1)
            for h in heads:
                hi, lo = _split_bf16(log_1mb[h])
                s.append(jnp.dot(jnp.concatenate([hi, lo], axis=1), sums2,
                                 preferred_element_type=F32))
            for h in heads:
                tail = tail_refs[h][rows, :]
                a = jnp.exp(log_b[h] + s[h][:, :blk] + tail)
                if masked:
                    a = jnp.where(strict, a, 0.0)
                acc_refs[h][rows, :] += jnp.dot(a.astype(BF16), v_ref[pl.ds(start, blk), lanes[h]],
                                                preferred_element_type=F32)
                tail_refs[h][rows, :] = tail + s[h][:, blk:]
            return carry
        return step

    diagonal = make_step(qt * per_tile + per_tile - 1, True)
    for i in range(per_tile):
        diagonal(i, 0)
    lax.fori_loop(0, qt * per_tile, make_step(qt * per_tile - 1, False), 0)
    for h in range(SB_HEADS_PER_STEP):
        o_ref[:, h * hd:(h + 1) * hd] = acc_refs[h][...].astype(o_ref.dtype)


def stick_breaking_attention(q, kv, batch, seq):
    t, width = q.shape
    nq = seq // SB_Q_TILE
    gw = SB_HEADS_PER_STEP * SB_HEAD_DIM
    n_groups = width // gw
    return pl.pallas_call(
        functools.partial(_attn_body, scale=1.0 / math.sqrt(SB_HEAD_DIM)),
        grid=(batch, n_groups, nq),
        in_specs=[
            pl.BlockSpec((SB_Q_TILE, gw), lambda b, g, i: (b * nq + i, g)),
            pl.BlockSpec((seq, gw), lambda b, g, i: (b, g)),
            pl.BlockSpec((seq, gw), lambda b, g, i: (b, n_groups + g)),
        ],
        out_specs=pl.BlockSpec((SB_Q_TILE, gw), lambda b, g, i: (b * nq + i, g)),
        out_shape=jax.ShapeDtypeStruct((t, width), BF16),
        scratch_shapes=([pltpu.VMEM((SB_Q_TILE, SB_HEAD_DIM), F32)] * SB_HEADS_PER_STEP
                        + [pltpu.VMEM((SB_Q_TILE, SB_BLOCK), F32)] * SB_HEADS_PER_STEP),
        compiler_params=_params("arbitrary", "arbitrary", "arbitrary"),
        name="stick_breaking_attention",
    )(q, kv, kv)


def _take_best(s_ref, ids, v_ref, i_ref, e):
    s = s_ref[...]
    m = jnp.max(s, axis=0, keepdims=True)
    first = jnp.min(jnp.where(s == m, ids, jnp.inf), axis=0, keepdims=True)
    s_ref[...] = jnp.where(ids == first, -jnp.inf, s)
    v_ref[e:e + 1, :] = m
    i_ref[e:e + 1, :] = first


def _pair_candidates(v1, v2):
    k = PEER_TOPK
    tokens = v1.shape[1]
    sub = lax.broadcasted_iota(jnp.int32, (SUBLANES, tokens), 0)
    sub16 = lax.broadcasted_iota(jnp.int32, (k, tokens), 0)
    vals, ids = [], []
    vals.append(v1[0:1, :] + v2)
    ids.append(sub16)
    vals.append(jnp.where(sub16 >= 1, v1 + v2[0:1, :], -jnp.inf))
    ids.append(sub16 * k)
    for a in range(1, 5):
        b_max = k // (a + 1) - 1
        vals.append(jnp.where((sub >= 1) & (sub <= b_max), v1[a:a + 1, :] + v2[0:SUBLANES, :], -jnp.inf))
        ids.append(sub + a * k)
    vals.append(jnp.where((sub >= 5) & (sub <= 7), v1[0:SUBLANES, :] + v2[1:2, :], -jnp.inf))
    ids.append(sub * k + 1)
    return jnp.concatenate(vals, axis=0), jnp.concatenate(ids, axis=0).astype(F32)


PAIR_ROWS = 2 * PEER_TOPK + 5 * SUBLANES
ROUTE_KINDS = 4
ROUTE_PIECES = ROUTE_KINDS * PEER_HEADS


def _route_step(kind, e, head, slot, q_ref, k1_ref, k2_ref, idx_t_ref, gate_t_ref,
                s_ref, v_ref, i_ref, cand_ref, cid_ref, sc_ref, ci_ref):
    if kind == 0:
        if e == 0:
            nt = (((1,), (1,)), ((), ()))
            q = q_ref[head]
            s_ref[0] = lax.dot_general(k1_ref[...], q[:, :PEER_HALF], nt, preferred_element_type=F32,
                                       precision=HIGHEST)
            s_ref[1] = lax.dot_general(k2_ref[...], q[:, PEER_HALF:], nt, preferred_element_type=F32,
                                       precision=HIGHEST)
        return
    if kind in (1, 2):
        half = kind - 1
        key_ids = lax.broadcasted_iota(jnp.int32, s_ref.shape[1:], 0).astype(F32)
        _take_best(s_ref.at[half], key_ids, v_ref.at[half], i_ref.at[half], e)
        return
    if e == 0:
        cand_ref[...], cid_ref[...] = _pair_candidates(v_ref[0], v_ref[1])
    _take_best(cand_ref, cid_ref[...], sc_ref, ci_ref, e)
    if e == PEER_TOPK - 1:
        sc = sc_ref[...]
        ci = ci_ref[...].astype(jnp.int32)
        i1 = i_ref[0].astype(jnp.int32)
        i2 = i_ref[1].astype(jnp.int32)
        ca = ci >> 4
        cbi = ci & (PEER_TOPK - 1)
        e1 = jnp.zeros_like(ci)
        e2 = jnp.zeros_like(ci)
        for a in range(PEER_TOPK):
            e1 = jnp.where(ca == a, i1[a:a + 1, :], e1)
            e2 = jnp.where(cbi == a, i2[a:a + 1, :], e2)
        p = jnp.exp(sc - jnp.max(sc, axis=0, keepdims=True))
        gate = p / jnp.sum(p, axis=0, keepdims=True)
        rows = pl.ds(pl.multiple_of(head * PEER_TOPK, PEER_TOPK), PEER_TOPK)
        idx_t_ref[slot, rows, :] = e1 * PEER_N_KEYS + e2
        gate_t_ref[slot, rows, :] = gate


def _pack_pairs(x):
    half = x.shape[1] // 2
    bits = lax.bitcast_convert_type(x.astype(BF16).astype(F32), U32)
    return (bits[:, :half] >> 16) | (bits[:, half:] & jnp.uint32(0xFFFF0000))


def _unpack_pairs(w):
    lo = lax.bitcast_convert_type(w << 16, F32)
    hi = lax.bitcast_convert_type(w & jnp.uint32(0xFFFF0000), F32)
    return lo, hi


def _pack_tables_body(u_ref, v_ref, o_ref):
    half_chunks = o_ref.shape[1] // 2
    for t, tab_ref in enumerate((u_ref, v_ref)):
        packed = _pack_pairs(tab_ref[...])
        for c in range(half_chunks):
            o_ref[:, t * half_chunks + c, 0, :] = packed[:, c * LANES:(c + 1) * LANES]


def pack_tables(u_tabs, v_tabs, layer, tm=256):
    _, n, d = u_tabs.shape
    n_chunks = d // LANES
    layer_rows = pl.BlockSpec((None, tm, d), lambda i: (layer, i, 0))
    return pl.pallas_call(
        _pack_tables_body,
        grid=(n // tm,),
        in_specs=[layer_rows, layer_rows],
        out_specs=pl.BlockSpec((tm, n_chunks, 1, LANES), lambda i: (i, 0, 0, 0)),
        out_shape=jax.ShapeDtypeStruct((n, n_chunks, 1, LANES), U32),
        compiler_params=_params("arbitrary"),
        name="peer_pack_tables",
    )(u_tabs, v_tabs)


def _gelu(x):
    return 0.5 * x * (1.0 + lax.erf(x * (1.0 / math.sqrt(2.0))))


def _peer_body(xb0_ref, xbn_ref, wq_ref, k1_ref, k2_ref, x_ref, h_ref, pg_ref, tab_hbm, o_ref, *rest,
               tb, n_blocks, post_gain_of_out):
    n_post = len(post_gain_of_out)
    post_refs = rest[:n_post]
    (q_ring, idx_t_ref, gate_t_ref, idx_rows, gate_rows, idx_smem, buf, mix_ref, idx_sem,
     row_sem) = rest[n_post:n_post + 10]
    route_scratch = rest[n_post + 10:]
    step = pl.program_id(0)
    blk = step - PEER_LEAD
    mixing = blk >= 0
    n_groups = tb // SUBLANES
    route_slot = lax.rem(step, PEER_RING)
    q_cur = q_ring.at[lax.rem(step, 2)]
    q_next = q_ring.at[lax.rem(step + 1, 2)]
    kc_w = wq_ref.shape[1] // (2 * ROUTE_KINDS)

    @pl.when(step == 0)
    def _():
        for hd in range(PEER_HEADS):
            q_cur[hd] = jnp.dot(xb0_ref[...], wq_ref[hd], preferred_element_type=F32)

    def project(kind, head, u):
        kc = 2 * kind + u
        ks = slice(kc * kc_w, (kc + 1) * kc_w)
        part = jnp.dot(xbn_ref[:, ks], wq_ref[head, ks, :], preferred_element_type=F32)
        if kc == 0:
            q_next[head] = part
        else:
            q_next[head] += part

    def route_step(kind, e, head):
        _route_step(kind, e, head, route_slot, q_cur, k1_ref, k2_ref, idx_t_ref, gate_t_ref, *route_scratch)

    def route_piece(kind, head):
        for u in range(2):
            project(kind, head, u)
        for e in range(PEER_TOPK):
            route_step(kind, e, head)

    def row_copy(j, slot, k):
        e = idx_smem[j, k]
        return pltpu.make_async_copy(tab_hbm.at[e], buf.at[slot, :, pl.ds(k, 1), :], row_sem.at[slot])

    def wait(slot):
        pltpu.make_async_copy(buf.at[slot], buf.at[slot], row_sem.at[slot]).wait()

    @pl.when(mixing)
    def _():
        slot = lax.rem(blk, PEER_RING)
        nxt = lax.rem(blk + 1, PEER_RING)
        for j in range(tb // LANES):
            cols = slice(j * LANES, (j + 1) * LANES)
            gate_rows[cols, :] = gate_t_ref[slot, :, cols].T
            idx_rows[cols, :] = idx_t_ref[slot, :, cols].astype(F32).T.astype(jnp.int32)
        head_rows = idx_t_ref[nxt, :, 0:LANES].astype(F32).T.astype(jnp.int32)
        idx_rows[tb:tb + PEER_BUFS, :] = head_rows[0:PEER_BUFS, :]
        ids = pltpu.make_async_copy(idx_rows, idx_smem, idx_sem)
        ids.start()
        ids.wait()

    @pl.when(blk == 0)
    def _():
        for j in range(PEER_AHEAD):
            def body(k, carry):
                row_copy(j, j, k).start()
                return carry
            lax.fori_loop(0, PEER_SLOTS, body, 0, unroll=8)

    d = x_ref.shape[1]
    half_chunks = d // (2 * LANES)
    row = lax.broadcasted_iota(jnp.int32, (PEER_SLOTS, LANES), 0)
    col = lax.broadcasted_iota(jnp.int32, (PEER_SLOTS, LANES), 1)
    eye = row == col
    per_chunk = PEER_SLOTS // (2 * half_chunks)

    def mix_group(g, kind):
        base = pl.multiple_of(g * SUBLANES, SUBLANES)
        rows = pl.ds(base, SUBLANES)
        gate8 = gate_rows[rows, :]
        head = g // ROUTE_KINDS
        steps_per_token = PEER_TOPK // SUBLANES

        for r in range(SUBLANES):
            ahead_slot = (r + PEER_AHEAD) % PEER_BUFS
            started = [0]

            def start_some():
                for k in range(started[0], started[0] + per_chunk):
                    row_copy(base + r + PEER_AHEAD, ahead_slot, k).start(priority=k % 2)
                started[0] += per_chunk

            wait(r)
            acc = jnp.zeros((PEER_SLOTS, LANES), F32)
            for c in range(half_chunks):
                lo, hi = _unpack_pairs(buf[r, c])
                x_lo = x_ref[rows, c * LANES:(c + 1) * LANES][r:r + 1, :]
                x_hi = x_ref[rows, (half_chunks + c) * LANES:(half_chunks + c + 1) * LANES][r:r + 1, :]
                acc = acc + lo * x_lo + hi * x_hi
                start_some()
                if c == 0 and r in (2, 5):
                    project(kind, head, r // 3)
                if c == half_chunks // 2:
                    route_step(kind, steps_per_token * r, head)
            hid = jnp.sum(acc, axis=1, keepdims=True)
            gate_row = jnp.broadcast_to(gate8[r:r + 1, :], (PEER_SLOTS, LANES))
            gate = jnp.sum(jnp.where(eye, gate_row, 0.0), axis=1, keepdims=True)
            w = jnp.broadcast_to(gate * _gelu(hid), (PEER_SLOTS, LANES))
            for c in range(half_chunks):
                lo, hi = _unpack_pairs(buf[r, half_chunks + c])
                mix_ref[r:r + 1, c * LANES:(c + 1) * LANES] = jnp.sum(lo * w, axis=0, keepdims=True)
                mix_ref[r:r + 1, (half_chunks + c) * LANES:(half_chunks + c + 1) * LANES] = (
                    jnp.sum(hi * w, axis=0, keepdims=True))
                start_some()
                if c == half_chunks // 2:
                    for e in range(steps_per_token * r + 1, steps_per_token * (r + 1)):
                        route_step(kind, e, head)
        y = h_ref[rows, :] + mix_ref[...]
        o_ref[rows, :] = y
        if post_refs:
            yn = y * lax.rsqrt(jnp.mean(y * y, axis=-1, keepdims=True) + NORM_EPS)
            for p_ref, gi in zip(post_refs, post_gain_of_out):
                p_ref[rows, :] = (yn * pg_ref[gi:gi + 1, :]).astype(p_ref.dtype)

    def group(g, carry):
        kind = lax.rem(g, ROUTE_KINDS)

        @pl.when(mixing)
        def _():
            lax.switch(kind, [functools.partial(mix_group, kind=k) for k in range(ROUTE_KINDS)], g)

        @pl.when(jnp.logical_not(mixing))
        def _():
            lax.switch(kind, [functools.partial(route_piece, k) for k in range(ROUTE_KINDS)], g // ROUTE_KINDS)
        return carry

    lax.fori_loop(0, n_groups, group, 0)

    @pl.when(blk == n_blocks - 1)
    def _():
        for j in range(PEER_AHEAD):
            wait(j)


def peer_route_and_mix(xn, xn_b, w_q, sub_keys, h, table, post_gains, post_outs, tb=256):
    t, d = xn.shape
    n_blocks = t // tb
    n_chunks = d // LANES
    qw = 2 * PEER_HALF
    assert PEER_BUFS == SUBLANES and PEER_AHEAD < PEER_BUFS and PEER_SLOTS % n_chunks == 0
    assert tb % LANES == 0 and tb // SUBLANES == ROUTE_PIECES and n_blocks * tb == t
    assert table.shape[1:] == (n_chunks, 1, LANES) and w_q.shape == (d, PEER_HEADS * qw)
    wq_heads = w_q.astype(BF16).reshape(d, PEER_HEADS, qw).transpose(1, 0, 2)
    pg = jnp.stack(post_gains)
    mixed = lambda s: (jnp.maximum(s - PEER_LEAD, 0), 0)
    keys = pl.BlockSpec((PEER_N_KEYS, PEER_HALF), lambda s: (0, 0))
    row_blk = pl.BlockSpec((tb, d), mixed)
    return pl.pallas_call(
        functools.partial(_peer_body, tb=tb, n_blocks=n_blocks, post_gain_of_out=tuple(gi for gi, _ in post_outs)),
        grid=(n_blocks + PEER_LEAD,),
        in_specs=[
            pl.BlockSpec((tb, d), lambda s: (0, 0)),
            pl.BlockSpec((tb, d), lambda s: (jnp.minimum(s + 1, n_blocks - 1), 0)),
            pl.BlockSpec((PEER_HEADS, d, qw), lambda s: (0, 0, 0)),
            keys, keys,
            row_blk,
            row_blk,
            pl.BlockSpec(pg.shape, lambda s: (0, 0)),
            pl.BlockSpec(memory_space=pl.ANY),
        ],
        out_specs=[row_blk] + [row_blk for _ in post_outs],
        out_shape=[jax.ShapeDtypeStruct((t, d), F32)] + [jax.ShapeDtypeStruct((t, d), dt) for _, dt in post_outs],
        scratch_shapes=[
            pltpu.VMEM((2, PEER_HEADS, tb, qw), F32),
            pltpu.VMEM((PEER_RING, PEER_SLOTS, tb), jnp.int32),
            pltpu.VMEM((PEER_RING, PEER_SLOTS, tb), F32),
            pltpu.VMEM((tb + PEER_BUFS, PEER_SLOTS), jnp.int32),
            pltpu.VMEM((tb, PEER_SLOTS), F32),
            pltpu.SMEM((tb + PEER_BUFS, PEER_SLOTS), jnp.int32),
            pltpu.VMEM((PEER_BUFS, n_chunks, PEER_SLOTS, LANES), U32),
            pltpu.VMEM((SUBLANES, d), F32),
            pltpu.SemaphoreType.DMA(()),
            pltpu.SemaphoreType.DMA((PEER_BUFS,)),
            pltpu.VMEM((2, PEER_N_KEYS, tb), F32),
            pltpu.VMEM((2, PEER_TOPK, tb), F32), pltpu.VMEM((2, PEER_TOPK, tb), F32),
            pltpu.VMEM((PAIR_ROWS, tb), F32), pltpu.VMEM((PAIR_ROWS, tb), F32),
            pltpu.VMEM((PEER_TOPK, tb), F32), pltpu.VMEM((PEER_TOPK, tb), F32),
        ],
        compiler_params=pltpu.CompilerParams(dimension_semantics=("arbitrary",), vmem_limit_bytes=PEER_VMEM_LIMIT),
        name="peer_route_mix",
    )(xn_b, xn_b, wq_heads, sub_keys[0], sub_keys[1], xn, h, pg, table)


def peer_layer(h, norm_g, w_q, sub_keys, u_tabs, v_tabs, layer, post_gains, post_outs):
    xn, xn_b = rmsnorm(h, [norm_g], [(0, F32), (0, BF16)])
    return peer_route_and_mix(xn, xn_b, w_q, sub_keys, h, pack_tables(u_tabs, v_tabs, layer), post_gains, post_outs)


def kernel(x, norm_mix, norm_ffn, norm_final, m_in_proj, m_conv_w, m_conv_b, m_dt_bias, m_a_log, m_d, m_norm,
           m_out_proj, kv_norm, w_kv, sb_w_q, sb_w_o, peer_w_q, peer_sub_keys, peer_u, peer_v):
    batch, seq, d_model = x.shape
    t = batch * seq
    h = x.reshape(t, d_model)

    n_heads = m_dt_bias.shape[1]
    proj_w = m_in_proj.shape[2] - n_heads
    u, = rmsnorm(h, [norm_mix[0]], [(0, BF16)])
    w_in = m_in_proj[0]
    zx = matmul(u, w_in, n=proj_w, name="in_proj")
    w_dt = jnp.pad(w_in[:, proj_w:], ((0, 0), (0, LANES - n_heads)))
    dtr = matmul(u, w_dt, name="in_proj_dt")
    y = mamba_ssd(zx, dtr, m_conv_w[0], m_conv_b[0], m_dt_bias[0], m_a_log[0], m_d[0], m_norm[0], batch, seq)
    h = matmul(y, m_out_proj[0], residual=h, tn=512, name="out_proj")
    h, kvn, qn = peer_layer(h, norm_ffn[0], peer_w_q[0], peer_sub_keys[0], peer_u, peer_v, 0,
                            [kv_norm, norm_mix[1]], [(0, BF16), (1, BF16)])

    kv = matmul(kvn, w_kv, out_dtype=BF16, name="kv_proj")
    q = matmul(qn, sb_w_q[0], name="q_proj")
    o = stick_breaking_attention(q, kv, batch, seq)
    h = matmul(o, sb_w_o[0], residual=h, name="o_proj")
    _, out = peer_layer(h, norm_ffn[1], peer_w_q[1], peer_sub_keys[1], peer_u, peer_v, 1,
                        [norm_final], [(0, F32)])
    return out.reshape(batch, seq, d_model)
```
